```python
import jax, jax.numpy as jnp
from jax import lax
import numpy as np

D_MODEL = 2048
BATCH = 2
SEQ = 8192
DEPTH = 2

N_MIXERS = 2
N_A_LAYERS = (DEPTH + 1) // 2
N_B_LAYERS = DEPTH // 2
ROPE_THETA = 10000.0
EPS = 1e-6

MLA_HEADS = 16
MLA_Q_RANK = 512
MLA_KV_RANK = 512
MLA_NOPE = 128
MLA_ROPE = 64
MLA_QK = MLA_NOPE + MLA_ROPE
MLA_V = 128
MLA_IN = MLA_Q_RANK + MLA_KV_RANK + MLA_ROPE
Q_BLOCK = 128

DIL_GROUPS = ((128, 1), (512, 4), (2048, 16))
N_DIL_GROUPS = len(DIL_GROUPS)
DIL_HEADS = 16
DIL_HEAD_DIM = 128
DIL_GROUP_W = DIL_HEADS * DIL_HEAD_DIM
DIL_IN = N_DIL_GROUPS * 3 * DIL_GROUP_W

PEER_HEADS = 8
PEER_NKEYS = 128
PEER_EXPERTS = PEER_NKEYS * PEER_NKEYS
PEER_QDIM = 256
PEER_HALF = PEER_QDIM // 2
PEER_TOPK = 16
PEER_CHUNK = 128

kernel_name = 'hybrid_mla_dilated_peer'


def rms_norm(x, gain):
    xf = x.astype(jnp.float32)
    y = xf * lax.rsqrt(jnp.mean(xf * xf, axis=-1, keepdims=True) + EPS)
    return (y * gain.astype(jnp.float32)).astype(x.dtype)


def rope(x, positions):
    half = x.shape[-1] // 2
    inv_freq = ROPE_THETA ** (-jnp.arange(half, dtype=jnp.float32) / half)
    ang = positions.astype(jnp.float32)[..., None] * inv_freq
    ang = ang.reshape(ang.shape[:2] + (1,) * (x.ndim - 3) + (half,))
    cos, sin = jnp.cos(ang), jnp.sin(ang)
    xf = x.astype(jnp.float32)
    x1, x2 = xf[..., :half], xf[..., half:]
    return jnp.concatenate([x1 * cos - x2 * sin, x2 * cos + x1 * sin], axis=-1).astype(x.dtype)


def causal_block_attention(q, k, v, scale):
    B, S, H, Dq = q.shape
    nb = S // Q_BLOCK
    qb = q.reshape(B, nb, Q_BLOCK, H, Dq).transpose(1, 0, 2, 3, 4)
    k_pos = jnp.arange(S)

    def one_block(args):
        qi, bi = args
        s = jnp.einsum('bqhd,bkhd->bhqk', qi, k, preferred_element_type=jnp.float32) * scale
        q_pos = bi * Q_BLOCK + jnp.arange(Q_BLOCK)
        s = jnp.where(k_pos[None, :] <= q_pos[:, None], s, -jnp.inf)
        p = jax.nn.softmax(s, axis=-1)
        return jnp.einsum('bhqk,bkhd->bqhd', p.astype(v.dtype), v)

    o = lax.map(one_block, (qb, jnp.arange(nb)))
    return o.transpose(1, 0, 2, 3, 4).reshape(B, S, H, v.shape[-1])


def mla_mixer(h, positions, w_in, g_q, w_uq, g_kv, w_ukv, g_qn, g_kn, w_o):
    B, S, _ = h.shape
    z = h @ w_in
    c_q, c_kv, k_r = jnp.split(z, [MLA_Q_RANK, MLA_Q_RANK + MLA_KV_RANK], axis=-1)
    q = (rms_norm(c_q, g_q) @ w_uq).reshape(B, S, MLA_HEADS, MLA_QK)
    kv = (rms_norm(c_kv, g_kv) @ w_ukv).reshape(B, S, MLA_HEADS, MLA_NOPE + MLA_V)
    k_nope, v = kv[..., :MLA_NOPE], kv[..., MLA_NOPE:]
    k_rope = jnp.broadcast_to(k_r[:, :, None, :], (B, S, MLA_HEADS, MLA_ROPE))
    k = jnp.concatenate([k_nope, k_rope], axis=-1)
    q = rms_norm(q, g_qn)
    k = rms_norm(k, g_kn)
    q = jnp.concatenate([q[..., :MLA_NOPE], rope(q[..., MLA_NOPE:], positions)], axis=-1)
    k = jnp.concatenate([k[..., :MLA_NOPE], rope(k[..., MLA_NOPE:], positions)], axis=-1)
    o = causal_block_attention(q, k, v, MLA_QK ** -0.5)
    return o.reshape(B, S, MLA_HEADS * MLA_V) @ w_o


def dilated_group_attention(q, k, v, window, dilation):
    B, S, H, dh = q.shape
    steps = window // dilation
    L = S // dilation
    nb = -(-L // steps)
    Lp = nb * steps

    def to_sub(t):
        t = t.reshape(B, L, dilation, H, dh).transpose(0, 2, 3, 1, 4)
        t = jnp.pad(t, ((0, 0), (0, 0), (0, 0), (0, Lp - L), (0, 0)))
        return t.reshape(B, dilation, H, nb, steps, dh)

    def with_prev(t):
        prev = jnp.pad(t, ((0, 0), (0, 0), (0, 0), (1, 0), (0, 0), (0, 0)))[:, :, :, :-1]
        return jnp.concatenate([prev, t], axis=4)

    qs, ks, vs = to_sub(q), to_sub(k), to_sub(v)
    kb, vb = with_prev(ks), with_prev(vs)
    s = jnp.einsum('brhnqd,brhnkd->brhnqk', qs, kb, preferred_element_type=jnp.float32) * dh ** -0.5
    qi = jnp.arange(steps)[:, None]
    kj = jnp.arange(2 * steps)[None, :]
    dist = qi + steps - kj
    blk = jnp.arange(nb)[:, None, None]
    valid = (dist >= 0) & (dist <= steps) & (blk * steps + kj - steps >= 0)
    s = jnp.where(valid, s, -jnp.inf)
    lse = jax.nn.logsumexp(s, axis=-1)
    p = jnp.exp(s - lse[..., None])
    o = jnp.einsum('brhnqk,brhnkd->brhnqd', p.astype(v.dtype), vb)
    o = o.reshape(B, dilation, H, Lp, dh)[:, :, :, :L].transpose(0, 3, 1, 2, 4).reshape(B, S, H, dh)
    lse = lse.reshape(B, dilation, H, Lp)[:, :, :, :L].transpose(0, 3, 1, 2).reshape(B, S, H)
    return o, lse


def dilated_mixer(h, positions, w_in, g_qn, g_kn, w_o):
    B, S, _ = h.shape
    z = (h @ w_in).reshape(B, S, N_DIL_GROUPS, 3, DIL_HEADS, DIL_HEAD_DIM)
    q_all = rope(rms_norm(z[:, :, :, 0], g_qn[:, None, :]), positions)
    k_all = rope(rms_norm(z[:, :, :, 1], g_kn[:, None, :]), positions)
    v_all = z[:, :, :, 2]
    outs, lses = [], []
    for gi, (window, dilation) in enumerate(DIL_GROUPS):
        o, lse = dilated_group_attention(q_all[:, :, gi], k_all[:, :, gi], v_all[:, :, gi], window, dilation)
        outs.append(o)
        lses.append(lse)
    w = jax.nn.softmax(jnp.stack(lses, axis=0), axis=0)
    o = jnp.einsum('gbsh,gbshd->bshd', w.astype(v_all.dtype), jnp.stack(outs, axis=0))
    return o.reshape(B, S, DIL_GROUP_W) @ w_o


def peer_ffn(h, w_q, sub_keys, u_tab, v_tab):
    B, S, D = h.shape
    T = B * S
    x = h.reshape(T, D)
    q = (x @ w_q).reshape(T, PEER_HEADS, 2, PEER_HALF)
    s = jnp.einsum('thpc,hpnc->thpn', q, sub_keys, preferred_element_type=jnp.float32)
    top_s, top_i = lax.top_k(s, PEER_TOPK)
    cand = top_s[:, :, 0, :, None] + top_s[:, :, 1, None, :]
    best_s, best_c = lax.top_k(cand.reshape(T, PEER_HEADS, PEER_TOPK * PEER_TOPK), PEER_TOPK)
    i1 = jnp.take_along_axis(top_i[:, :, 0], best_c // PEER_TOPK, axis=-1)
    i2 = jnp.take_along_axis(top_i[:, :, 1], best_c % PEER_TOPK, axis=-1)
    idx = (i1 * PEER_NKEYS + i2).reshape(T, PEER_HEADS * PEER_TOPK)
    gate = jax.nn.softmax(best_s, axis=-1).reshape(T, PEER_HEADS * PEER_TOPK)
    nc = T // PEER_CHUNK

    def chunk(args):
        xc, ic, gc = args
        u = jnp.take(u_tab, ic, axis=0)
        a = jnp.einsum('cd,ced->ce', xc, u, preferred_element_type=jnp.float32)
        act = (jax.nn.gelu(a, approximate=False) * gc).astype(xc.dtype)
        return jnp.einsum('ce,ced->cd', act, jnp.take(v_tab, ic, axis=0))

    y = lax.map(chunk, (x.reshape(nc, PEER_CHUNK, D), idx.reshape(nc, PEER_CHUNK, -1), gate.reshape(nc, PEER_CHUNK, -1)))
    return y.reshape(B, S, D)


def setup_inputs(seed: int = 0) -> dict:
    key = jax.random.key(seed)
    ks = jax.random.split(key, 24)
    f32 = jnp.float32
    D = D_MODEL

    def nrm(k, shape, scale):
        return jax.random.normal(k, shape, f32) * scale

    def gain(k, shape):
        return 1.0 + 0.05 * jax.random.normal(k, shape, f32)

    return {
        'x': nrm(ks[0], (BATCH, SEQ, D), 1.0),
        'c': nrm(ks[1], (BATCH, D), 1.0),
        'positions': jnp.arange(SEQ, dtype=jnp.int32)[None, :] + jax.random.randint(ks[2], (BATCH, 1), 0, 4096, dtype=jnp.int32),
        'ada_w': nrm(ks[3], (DEPTH, D, 6 * D), 0.5 * D ** -0.5),
        'ada_b': nrm(ks[4], (DEPTH, 6 * D), 0.01),
        'norm_g': gain(ks[5], (DEPTH, 2, D)),
        'mla_w_in': nrm(ks[6], (N_A_LAYERS, D, MLA_IN), D ** -0.5),
        'mla_g_q': gain(ks[7], (N_A_LAYERS, MLA_Q_RANK)),
        'mla_w_uq': nrm(ks[8], (N_A_LAYERS, MLA_Q_RANK, MLA_HEADS * MLA_QK), MLA_Q_RANK ** -0.5),
        'mla_g_kv': gain(ks[9], (N_A_LAYERS, MLA_KV_RANK)),
        'mla_w_ukv': nrm(ks[10], (N_A_LAYERS, MLA_KV_RANK, MLA_HEADS * (MLA_NOPE + MLA_V)), MLA_KV_RANK ** -0.5),
        'mla_g_qn': gain(ks[11], (N_A_LAYERS, MLA_QK)),
        'mla_g_kn': gain(ks[12], (N_A_LAYERS, MLA_QK)),
        'mla_w_o': nrm(ks[13], (N_A_LAYERS, MLA_HEADS * MLA_V, D), (MLA_HEADS * MLA_V) ** -0.5),
        'dil_w_in': nrm(ks[14], (N_B_LAYERS, D, DIL_IN), D ** -0.5),
        'dil_g_qn': gain(ks[15], (N_B_LAYERS, N_DIL_GROUPS, DIL_HEAD_DIM)),
        'dil_g_kn': gain(ks[16], (N_B_LAYERS, N_DIL_GROUPS, DIL_HEAD_DIM)),
        'dil_w_o': nrm(ks[17], (N_B_LAYERS, DIL_GROUP_W, D), DIL_GROUP_W ** -0.5),
        'peer_w_q': nrm(ks[18], (DEPTH, D, PEER_HEADS * PEER_QDIM), D ** -0.5),
        'peer_sub_keys': nrm(ks[19], (DEPTH, PEER_HEADS, 2, PEER_NKEYS, PEER_HALF), PEER_HALF ** -0.5),
        'peer_u': nrm(ks[20], (DEPTH, PEER_EXPERTS, D), D ** -0.5),
        'peer_v': nrm(ks[21], (DEPTH, PEER_EXPERTS, D), PEER_HEADS ** -0.5),
    }


def reference(x, c, positions, ada_w, ada_b, norm_g, mla_w_in, mla_g_q, mla_w_uq, mla_g_kv, mla_w_ukv, mla_g_qn, mla_g_kn, mla_w_o, dil_w_in, dil_g_qn, dil_g_kn, dil_w_o, peer_w_q, peer_sub_keys, peer_u, peer_v):
    mod = jnp.einsum('bd,lde->lbe', jax.nn.silu(c), ada_w) + ada_b[:, None, :]
    for layer in range(DEPTH):
        shift1, scale1, gate1, shift2, scale2, gate2 = jnp.split(mod[layer][:, None, :], 6, axis=-1)
        h = rms_norm(x, norm_g[layer, 0]) * (1.0 + scale1) + shift1
        if layer % N_MIXERS == 0:
            a = layer // N_MIXERS
            y = mla_mixer(h, positions, mla_w_in[a], mla_g_q[a], mla_w_uq[a], mla_g_kv[a], mla_w_ukv[a], mla_g_qn[a], mla_g_kn[a], mla_w_o[a])
        else:
            b = layer // N_MIXERS
            y = dilated_mixer(h, positions, dil_w_in[b], dil_g_qn[b], dil_g_kn[b], dil_w_o[b])
        x = x + gate1 * y
        h = rms_norm(x, norm_g[layer, 1]) * (1.0 + scale2) + shift2
        x = x + gate2 * peer_ffn(h, peer_w_q[layer], peer_sub_keys[layer], peer_u[layer], peer_v[layer])
    return x
```

```python
import functools
import math

import jax
import jax.numpy as jnp
from jax import lax
from jax.experimental import pallas as pl
from jax.experimental.pallas import tpu as pltpu

F32 = jnp.float32
BF16 = jnp.bfloat16
EPS = 1e-6
ROPE_THETA = 10000.0
LOG2E = 1.4426950408889634
NEG_INF = float("-inf")
LANES = 128

MLA_HEADS = 16
MLA_Q_RANK = 512
MLA_KV_RANK = 512
MLA_NOPE = 128
MLA_ROPE = 64
MLA_QK = MLA_NOPE + MLA_ROPE
MLA_V = 128
MLA_HEAD_PAD = 256

DIL_GROUPS = ((128, 1), (512, 4), (2048, 16))
DIL_HEADS = 16
DIL_HEAD_DIM = 128
DIL_STEPS = 128

PEER_HEADS = 8
PEER_NKEYS = 128
PEER_TOPK = 16

_NT = (((1,), (1,)), ((), ()))


def _params(sem, vmem_mb=48):
    return pltpu.CompilerParams(dimension_semantics=sem, vmem_limit_bytes=vmem_mb << 20)


def _ada_kernel(c_ref, w_ref, b_ref, o_ref):
    c = c_ref[...]
    sc = c / (1.0 + jnp.exp(-c))
    o_ref[0] = jnp.dot(sc, w_ref[0], preferred_element_type=F32,
                       precision=lax.Precision.HIGHEST) + b_ref[0]


def _ada_mod(c, ada_w, ada_b, tn=768):
    depth, d, n = ada_w.shape
    b = c.shape[0]
    return pl.pallas_call(
        _ada_kernel,
        name="ada_mod",
        grid=(depth, n // tn),
        in_specs=[pl.BlockSpec((b, d), lambda l, j: (0, 0)),
                  pl.BlockSpec((1, d, tn), lambda l, j: (l, 0, j)),
                  pl.BlockSpec((1, 1, tn), lambda l, j: (l, 0, j))],
        out_specs=pl.BlockSpec((1, b, tn), lambda l, j: (l, 0, j)),
        out_shape=jax.ShapeDtypeStruct((depth, b, n), F32),
        compiler_params=_params(("parallel", "parallel")),
    )(c, ada_w, ada_b.reshape(depth, 1, n))


def _rope_tab_kernel(pos_ref, fm_ref, fd_ref, mc_ref, ms1_ref, ms2_ref, dc_ref, ds_ref):
    pos = pos_ref[0].astype(F32)
    lane = lax.broadcasted_iota(jnp.int32, (pos.shape[0], LANES), 1)
    am = pos * fm_ref[...]
    cm, sm = jnp.cos(am), jnp.sin(am)
    mc_ref[0] = jnp.where(lane < 2 * (MLA_ROPE // 2), cm, 0.0)
    ms1_ref[0] = jnp.where(lane < MLA_ROPE // 2, -sm, 0.0)
    ms2_ref[0] = jnp.where(lane < MLA_ROPE // 2, 0.0, jnp.where(lane < MLA_ROPE, sm, 0.0))
    ad = pos * fd_ref[...]
    dc_ref[0] = jnp.cos(ad)
    sd = jnp.sin(ad)
    ds_ref[0] = jnp.where(lane < DIL_HEAD_DIM // 2, -sd, sd)


def _rope_tables(positions, ts=512):
    b, s = positions.shape
    hm, hd = MLA_ROPE // 2, DIL_HEAD_DIM // 2
    inv_m = ROPE_THETA ** (-jnp.arange(hm, dtype=F32) / hm)
    inv_d = ROPE_THETA ** (-jnp.arange(hd, dtype=F32) / hd)
    fm = jnp.concatenate([inv_m, inv_m, jnp.zeros((LANES - 2 * hm,), F32)]).reshape(1, LANES)
    fd = jnp.concatenate([inv_d, inv_d]).reshape(1, LANES)
    tab = jax.ShapeDtypeStruct((b, s, LANES), F32)
    row = pl.BlockSpec((1, ts, LANES), lambda bi, i: (bi, i, 0))
    frq = pl.BlockSpec((1, LANES), lambda bi, i: (0, 0))
    return pl.pallas_call(
        _rope_tab_kernel,
        name="rope_tables",
        grid=(b, s // ts),
        in_specs=[pl.BlockSpec((1, ts, 1), lambda bi, i: (bi, i, 0)), frq, frq],
        out_specs=[row] * 5,
        out_shape=[tab] * 5,
        compiler_params=_params(("parallel", "parallel")),
    )(positions.reshape(b, s, 1), fm, fd)


def _norm_matmul_kernel(x_ref, g_ref, sc_ref, sh_ref, w_ref, o_ref, h_ref):
    @pl.when(pl.program_id(2) == 0)
    def _():
        x = x_ref[0]
        inv = lax.rsqrt(jnp.mean(x * x, axis=-1, keepdims=True) + EPS)
        h = (x * inv) * g_ref[...]
        h_ref[...] = (h * (1.0 + sc_ref[0]) + sh_ref[0]).astype(BF16)

    o_ref[0] = jnp.dot(h_ref[...], w_ref[...], preferred_element_type=F32).astype(o_ref.dtype)


def _norm_mod_kernel(x_ref, g_ref, sc_ref, sh_ref, o_ref):
    x = x_ref[0]
    inv = lax.rsqrt(jnp.mean(x * x, axis=-1, keepdims=True) + EPS)
    h = (x * inv) * g_ref[...]
    o_ref[0] = (h * (1.0 + sc_ref[0]) + sh_ref[0]).astype(o_ref.dtype)


def _norm_mod(x, g_row, modr, row_scale, row_shift, tm=512):
    b, s, d = x.shape
    return pl.pallas_call(
        _norm_mod_kernel,
        name="norm_mod",
        grid=(b, s // tm),
        in_specs=[pl.BlockSpec((1, tm, d), lambda bi, i: (bi, i, 0)),
                  pl.BlockSpec((1, d), lambda bi, i: (0, 0)),
                  pl.BlockSpec((1, 1, d), lambda bi, i: (row_scale(bi), 0, 0)),
                  pl.BlockSpec((1, 1, d), lambda bi, i: (row_shift(bi), 0, 0))],
        out_specs=pl.BlockSpec((1, tm, d), lambda bi, i: (bi, i, 0)),
        out_shape=jax.ShapeDtypeStruct((b, s, d), BF16),
        compiler_params=_params(("parallel", "parallel")),
    )(x, g_row, modr, modr)


def _norm_matmul(x, g_row, modr, row_scale, row_shift, w, tm, tn, out_dtype):
    b, s, d = x.shape
    n = w.shape[1]
    return pl.pallas_call(
        _norm_matmul_kernel,
        name="norm_matmul",
        grid=(b, s // tm, n // tn),
        in_specs=[pl.BlockSpec((1, tm, d), lambda bi, i, j: (bi, i, 0)),
                  pl.BlockSpec((1, d), lambda bi, i, j: (0, 0)),
                  pl.BlockSpec((1, 1, d), lambda bi, i, j: (row_scale(bi), 0, 0)),
                  pl.BlockSpec((1, 1, d), lambda bi, i, j: (row_shift(bi), 0, 0)),
                  pl.BlockSpec((d, tn), lambda bi, i, j: (0, j))],
        out_specs=pl.BlockSpec((1, tm, tn), lambda bi, i, j: (bi, i, j)),
        out_shape=jax.ShapeDtypeStruct((b, s, n), out_dtype),
        scratch_shapes=[pltpu.VMEM((tm, d), BF16)],
        compiler_params=_params(("parallel", "parallel", "arbitrary")),
    )(x, g_row, modr, modr, w)


def _proj_res_kernel(a_ref, w_ref, x_ref, g_ref, o_ref):
    y = jnp.dot(a_ref[0], w_ref[...], preferred_element_type=F32)
    o_ref[0] = x_ref[0] + g_ref[0] * y


def _proj_residual(a, w, x, modr, row_gate, tm=512, tn=512):
    b, s, k = a.shape
    d = w.shape[1]
    return pl.pallas_call(
        _proj_res_kernel,
        name="proj_residual",
        grid=(b, s // tm, d // tn),
        in_specs=[pl.BlockSpec((1, tm, k), lambda bi, i, j: (bi, i, 0)),
                  pl.BlockSpec((k, tn), lambda bi, i, j: (0, j)),
                  pl.BlockSpec((1, tm, tn), lambda bi, i, j: (bi, i, j)),
                  pl.BlockSpec((1, 1, tn), lambda bi, i, j: (row_gate(bi), 0, j))],
        out_specs=pl.BlockSpec((1, tm, tn), lambda bi, i, j: (bi, i, j)),
        out_shape=jax.ShapeDtypeStruct((b, s, d), F32),
        compiler_params=_params(("parallel", "parallel", "parallel")),
    )(a, w, x, modr)


def _mla_prep_kernel(z_ref, gq_ref, gkv_ref, wuq_ref, wukv_ref, gqn_ref, gkn_ref,
                     c_ref, s1_ref, s2_ref, q_ref, k_ref, v_ref, *, q_scale):
    z = z_ref[0]
    cq = z[:, :MLA_Q_RANK]
    ckv = z[:, MLA_Q_RANK:MLA_Q_RANK + MLA_KV_RANK]
    kr = z[:, MLA_Q_RANK + MLA_KV_RANK:]

    def rms(t, g):
        return (t * lax.rsqrt(jnp.mean(t * t, axis=-1, keepdims=True) + EPS)) * g

    qa = jnp.dot(rms(cq, gq_ref[...]).astype(BF16), wuq_ref[...], preferred_element_type=F32)
    kva = jnp.dot(rms(ckv, gkv_ref[...]).astype(BF16), wukv_ref[...], preferred_element_type=F32)
    cos, sin_lo, sin_hi = c_ref[0], s1_ref[0], s2_ref[0]

    def rope(t):
        return (t * cos + pltpu.roll(t, LANES - MLA_ROPE // 2, 1) * sin_lo
                + pltpu.roll(t, MLA_ROPE // 2, 1) * sin_hi)

    gqn, gkn = gqn_ref[...], gkn_ref[...]
    kr_ss = jnp.sum(kr * kr, axis=-1, keepdims=True)
    kr_rot = rope(kr * gkn[:, MLA_NOPE:])
    for h in range(MLA_HEADS):
        lo = h * MLA_HEAD_PAD
        qh = qa[:, lo:lo + MLA_HEAD_PAD]
        inv = lax.rsqrt(jnp.sum(qh * qh, axis=-1, keepdims=True) * (1.0 / MLA_QK) + EPS) * q_scale
        q_ref[0, h, :, :MLA_NOPE] = ((qh[:, :MLA_NOPE] * gqn[:, :MLA_NOPE]) * inv).astype(BF16)
        q_ref[0, h, :, MLA_NOPE:] = (rope(qh[:, MLA_NOPE:] * gqn[:, MLA_NOPE:]) * inv).astype(BF16)
        kn = kva[:, lo:lo + MLA_NOPE]
        kss = jnp.sum(kn * kn, axis=-1, keepdims=True) + kr_ss
        kinv = lax.rsqrt(kss * (1.0 / MLA_QK) + EPS)
        k_ref[0, h, :, :MLA_NOPE] = ((kn * gkn[:, :MLA_NOPE]) * kinv).astype(BF16)
        k_ref[0, h, :, MLA_NOPE:] = (kr_rot * kinv).astype(BF16)
        v_ref[0, h] = kva[:, lo + MLA_NOPE:lo + MLA_HEAD_PAD].astype(BF16)


def _mla_prep(z, g_q, g_kv, w_uq_p, w_ukv, g_qn_p, g_kn_p, tabs, tm=256):
    b, s, zw = z.shape
    nh = MLA_HEADS
    row = lambda w: pl.BlockSpec((1, w), lambda bi, i: (0, 0))
    full = lambda a: pl.BlockSpec(a.shape, lambda bi, i: (0, 0))
    tab = pl.BlockSpec((1, tm, LANES), lambda bi, i: (bi, i, 0))
    head = lambda w: pl.BlockSpec((1, nh, tm, w), lambda bi, i: (bi, 0, i, 0))
    kern = functools.partial(_mla_prep_kernel, q_scale=MLA_QK ** -0.5 * LOG2E)
    return pl.pallas_call(
        kern,
        name="mla_prep",
        grid=(b, s // tm),
        in_specs=[pl.BlockSpec((1, tm, zw), lambda bi, i: (bi, i, 0)),
                  row(MLA_Q_RANK), row(MLA_KV_RANK), full(w_uq_p), full(w_ukv),
                  row(MLA_HEAD_PAD), row(MLA_HEAD_PAD), tab, tab, tab],
        out_specs=[head(MLA_HEAD_PAD), head(MLA_HEAD_PAD), head(MLA_V)],
        out_shape=[jax.ShapeDtypeStruct((b, nh, s, MLA_HEAD_PAD), BF16),
                   jax.ShapeDtypeStruct((b, nh, s, MLA_HEAD_PAD), BF16),
                   jax.ShapeDtypeStruct((b, nh, s, MLA_V), BF16)],
        compiler_params=_params(("parallel", "parallel")),
    )(z, g_q, g_kv, w_uq_p, w_ukv, g_qn_p, g_kn_p, *tabs)


def _flash_kernel(q_ref, k_ref, v_ref, o_ref, m_ref, l_ref, acc_ref, *, t):
    qi = pl.program_id(2)
    q = q_ref[0, 0]
    m_ref[...] = jnp.full(m_ref.shape, NEG_INF, F32)
    l_ref[...] = jnp.zeros(l_ref.shape, F32)
    acc_ref[...] = jnp.zeros(acc_ref.shape, F32)

    def step(j, diagonal):
        start = pl.multiple_of(j * t, t)
        k = k_ref[0, 0, pl.ds(start, t), :]
        v = v_ref[0, 0, pl.ds(start, t), :]
        s = lax.dot_general(q, k, _NT, preferred_element_type=F32)
        if diagonal:
            row = lax.broadcasted_iota(jnp.int32, (t, t), 0)
            col = lax.broadcasted_iota(jnp.int32, (t, t), 1)
            s = jnp.where(col <= row, s, NEG_INF)
        m_prev = m_ref[...]
        m_new = jnp.maximum(m_prev, jnp.max(s, axis=-1, keepdims=True))
        p = jnp.exp2(s - m_new)
        alpha = jnp.exp2(m_prev - m_new)
        l_ref[...] = alpha * l_ref[...] + jnp.sum(p, axis=-1, keepdims=True)
        acc_ref[...] = alpha * acc_ref[...] + jnp.dot(p.astype(BF16), v, preferred_element_type=F32)
        m_ref[...] = m_new

    def body(j, carry):
        step(j, False)
        return carry

    lax.fori_loop(0, qi, body, 0)
    step(qi, True)
    o_ref[0] = (acc_ref[...] / l_ref[...]).astype(o_ref.dtype)


def _flash_attention(q, k, v, t=512):
    b, nh, s, dq = q.shape
    dv = v.shape[-1]
    return pl.pallas_call(
        functools.partial(_flash_kernel, t=t),
        name="mla_flash",
        grid=(b, nh, s // t),
        in_specs=[pl.BlockSpec((1, 1, t, dq), lambda bi, h, i: (bi, h, i, 0)),
                  pl.BlockSpec((1, 1, s, dq), lambda bi, h, i: (bi, h, 0, 0)),
                  pl.BlockSpec((1, 1, s, dv), lambda bi, h, i: (bi, h, 0, 0))],
        out_specs=pl.BlockSpec((1, t, dv), lambda bi, h, i: (bi, i, h)),
        out_shape=jax.ShapeDtypeStruct((b, s, nh * dv), BF16),
        scratch_shapes=[pltpu.VMEM((t, 1), F32), pltpu.VMEM((t, 1), F32), pltpu.VMEM((t, dv), F32)],
        compiler_params=_params(("parallel", "parallel", "arbitrary")),
    )(q, k, v)


_DIL_TN = 512
_DIL_HPT = _DIL_TN // DIL_HEAD_DIM
_DIL_TPP = DIL_HEADS // _DIL_HPT


def _dil_proj_kernel(h_ref, w_ref, gain_ref, c_ref, s_ref, o_ref, *, q_scale):
    part = pl.program_id(3) // _DIL_TPP
    y = jnp.dot(h_ref[0], w_ref[...], preferred_element_type=F32)

    @pl.when(part == 2)
    def _():
        for hh in range(_DIL_HPT):
            o_ref[0, 0, 0, hh] = y[:, hh * LANES:(hh + 1) * LANES].astype(BF16)

    @pl.when(part < 2)
    def _():
        cos, sin = c_ref[0], s_ref[0]
        gain = gain_ref[0]
        post = jnp.where(part == 0, q_scale, 1.0)
        for hh in range(_DIL_HPT):
            yh = y[:, hh * LANES:(hh + 1) * LANES]
            inv = lax.rsqrt(jnp.mean(yh * yh, axis=-1, keepdims=True) + EPS)
            yn = (yh * inv) * gain
            rot = yn * cos + pltpu.roll(yn, DIL_HEAD_DIM // 2, 1) * sin
            o_ref[0, 0, 0, hh] = (rot * post).astype(BF16)


def _dil_project(h, w_in, gains, cos_t, sin_t, group, dil):
    b, s, d = h.shape
    l = s // dil
    tl = min(512, l)
    tiles = 3 * _DIL_TPP
    kern = functools.partial(_dil_proj_kernel, q_scale=DIL_HEAD_DIM ** -0.5 * LOG2E)
    return pl.pallas_call(
        kern,
        name=f"dil_proj_d{dil}",
        grid=(b, dil, l // tl, tiles),
        in_specs=[pl.BlockSpec((1, tl, d), lambda bi, r, i, j: (bi, i, r)),
                  pl.BlockSpec((d, _DIL_TN), lambda bi, r, i, j: (0, group * tiles + j)),
                  pl.BlockSpec((1, 1, LANES), lambda bi, r, i, j: (group * 3 + j // _DIL_TPP, 0, 0)),
                  pl.BlockSpec((1, tl, LANES), lambda bi, r, i, j: (bi, i, r)),
                  pl.BlockSpec((1, tl, LANES), lambda bi, r, i, j: (bi, i, r))],
        out_specs=pl.BlockSpec((1, 1, 1, _DIL_HPT, tl, LANES),
                               lambda bi, r, i, j: (bi, r, j // _DIL_TPP, j % _DIL_TPP, i, 0)),
        out_shape=jax.ShapeDtypeStruct((b, dil, 3, DIL_HEADS, l, LANES), BF16),
        compiler_params=_params(("parallel", "parallel", "parallel", "arbitrary")),
    )(h.reshape(b, l, dil * d), w_in, gains,
      cos_t.reshape(b, l, dil * LANES), sin_t.reshape(b, l, dil * LANES))


def _dil_attn_kernel(*refs, tq, has_state, is_last):
    q_ref, kc_ref, kp_ref, vc_ref, vp_ref = refs[:5]
    refs = refs[5:]
    if has_state:
        acc_in, m_in, l_in = refs[:3]
        refs = refs[3:]
    if is_last:
        (o_ref,) = refs
    else:
        acc_out, m_out, l_out = refs

    first = pl.program_id(2) == 0
    row = lax.broadcasted_iota(jnp.int32, (tq, tq), 0)
    col = lax.broadcasted_iota(jnp.int32, (tq, tq), 1)
    dist = row - col
    bias_c = jnp.where(dist >= 0, jnp.where(dist <= DIL_STEPS, 0.0, NEG_INF), NEG_INF)
    rowp = lax.broadcasted_iota(jnp.int32, (tq, DIL_STEPS), 0)
    colp = lax.broadcasted_iota(jnp.int32, (tq, DIL_STEPS), 1)
    bias_p = jnp.where(colp >= rowp, jnp.where(first, NEG_INF, 0.0), NEG_INF)
    lane = lax.broadcasted_iota(jnp.int32, (tq, LANES), 1)
    if has_state:
        m_old, l_old = m_in[0], l_in[0]
    m_tile = jnp.zeros((tq, LANES), F32)
    l_tile = jnp.zeros((tq, LANES), F32)

    for h in range(DIL_HEADS):
        q = q_ref[0, 0, 0, h]
        sc = lax.dot_general(q, kc_ref[0, 0, 0, h], _NT, preferred_element_type=F32) + bias_c
        sp = lax.dot_general(q, kp_ref[0, 0, 0, h], _NT, preferred_element_type=F32) + bias_p
        m_new = jnp.maximum(jnp.max(sc, axis=-1, keepdims=True), jnp.max(sp, axis=-1, keepdims=True))
        if has_state:
            m_prev = m_old[:, h:h + 1]
            m_new = jnp.maximum(m_new, m_prev)
            alpha = jnp.exp2(m_prev - m_new)
        pc = jnp.exp2(sc - m_new)
        pp = jnp.exp2(sp - m_new)
        l_new = jnp.sum(pc, axis=-1, keepdims=True) + jnp.sum(pp, axis=-1, keepdims=True)
        acc = (jnp.dot(pc.astype(BF16), vc_ref[0, 0, 0, h], preferred_element_type=F32)
               + jnp.dot(pp.astype(BF16), vp_ref[0, 0, 0, h], preferred_element_type=F32))
        cols = slice(h * LANES, (h + 1) * LANES)
        if has_state:
            l_new = l_new + alpha * l_old[:, h:h + 1]
            acc = acc + alpha * acc_in[0, :, cols]
        if is_last:
            o_ref[0, :, cols] = (acc / l_new).astype(o_ref.dtype)
        else:
            acc_out[0, :, cols] = acc
            m_tile = jnp.where(lane == h, m_new, m_tile)
            l_tile = jnp.where(lane == h, l_new, l_tile)
    if not is_last:
        m_out[0] = m_tile
        l_out[0] = l_tile


def _dil_attention(qkv, state, is_last, tq=256):
    b, dil, _, nh, l, dh = qkv.shape
    tq = min(tq, l)
    width = nh * dh
    ratio = tq // DIL_STEPS
    cur = lambda part: pl.BlockSpec((1, 1, 1, nh, tq, dh), lambda bi, r, i: (bi, r, part, 0, i, 0))
    prev = lambda part: pl.BlockSpec(
        (1, 1, 1, nh, DIL_STEPS, dh),
        lambda bi, r, i: (bi, r, part, 0, jnp.maximum(i * ratio - 1, 0), 0))
    wide = pl.BlockSpec((1, tq, width), lambda bi, r, i: (bi, i, r))
    thin = pl.BlockSpec((1, tq, LANES), lambda bi, r, i: (bi, i, r))
    in_specs = [cur(0), cur(1), prev(1), cur(2), prev(2)]
    args = [qkv] * 5
    if state is not None:
        in_specs += [wide, thin, thin]
        args += [state[0].reshape(b, l, dil * width), state[1].reshape(b, l, dil * LANES),
                 state[2].reshape(b, l, dil * LANES)]
    if is_last:
        out_specs = wide
        out_shape = jax.ShapeDtypeStruct((b, l, dil * width), BF16)
    else:
        out_specs = [wide, thin, thin]
        out_shape = [jax.ShapeDtypeStruct((b, l, dil * width), F32),
                     jax.ShapeDtypeStruct((b, l, dil * LANES), F32),
                     jax.ShapeDtypeStruct((b, l, dil * LANES), F32)]
    kern = functools.partial(_dil_attn_kernel, tq=tq, has_state=state is not None, is_last=is_last)
    out = pl.pallas_call(
        kern,
        name=f"dil_attn_d{dil}",
        grid=(b, dil, l // tq),
        in_specs=in_specs,
        out_specs=out_specs,
        out_shape=out_shape,
        compiler_params=_params(("parallel", "parallel", "parallel")),
    )(*args)
    s = l * dil
    if is_last:
        return out.reshape(b, s, width)
    return (out[0].reshape(b, s, width), out[1].reshape(b, s, LANES), out[2].reshape(b, s, LANES))


def _take_top(cur, count):
    nrows = cur.shape[0]
    rows = lax.broadcasted_iota(jnp.int32, cur.shape, 0)
    vals = []
    for _ in range(count):
        mx = jnp.max(cur, axis=0, keepdims=True)
        vals.append(mx)
        hit = jnp.min(jnp.where(cur == mx, rows, nrows), axis=0, keepdims=True)
        cur = jnp.where(rows == hit, NEG_INF, cur)
    return vals


def _peer_topk_kernel(q_ref, keys_ref, s_ref, e_ref, thr_ref):
    k = PEER_TOPK
    sub = lax.broadcasted_iota(jnp.int32, (8, q_ref.shape[0]), 0)
    for h in range(PEER_HEADS):
        tops = []
        for p in range(2):
            hp = 2 * h + p
            qs = q_ref[:, hp * PEER_NKEYS:(hp + 1) * PEER_NKEYS]
            st = lax.dot_general(keys_ref[hp], qs, _NT, preferred_element_type=F32,
                                 precision=lax.Precision.HIGHEST)
            s_ref[hp] = st
            tops.append(_take_top(st, k))
        a, bv = tops
        b_lo = jnp.concatenate(bv[:8], axis=0)
        b_hi = jnp.concatenate(bv[8:], axis=0)
        a_hi = jnp.concatenate(a[8:], axis=0)
        slabs = [a[0] + b_lo, a[0] + b_hi, a[1] + b_lo]
        for i in range(2, 8):
            slabs.append(jnp.where(sub < k // (i + 1), a[i] + b_lo, NEG_INF))
        slabs.append(a_hi + bv[0])
        best = _take_top(jnp.concatenate(slabs, axis=0), k)
        top = best[0]
        z = jnp.sum(jnp.exp(jnp.concatenate(best, axis=0) - top), axis=0, keepdims=True)
        thr_ref[h:h + 1, :] = best[k - 1]
        e_ref[2 * h] = jnp.exp(s_ref[2 * h] - a[0]) / z
        e_ref[2 * h + 1] = jnp.exp(s_ref[2 * h + 1] - bv[0])


def _peer_topk(q2d, sub_keys, tt=512):
    t = q2d.shape[0]
    nhp = 2 * PEER_HEADS
    keys = sub_keys.reshape(nhp, PEER_NKEYS, sub_keys.shape[-1])
    big = pl.BlockSpec((nhp, PEER_NKEYS, tt), lambda i: (0, 0, i))
    return pl.pallas_call(
        _peer_topk_kernel,
        name="peer_topk",
        grid=(t // tt,),
        in_specs=[pl.BlockSpec((tt, q2d.shape[1]), lambda i: (i, 0)),
                  pl.BlockSpec(keys.shape, lambda i: (0, 0, 0))],
        out_specs=[big, big, pl.BlockSpec((PEER_HEADS, tt), lambda i: (0, i))],
        out_shape=[jax.ShapeDtypeStruct((nhp, PEER_NKEYS, t), F32),
                   jax.ShapeDtypeStruct((nhp, PEER_NKEYS, t), F32),
                   jax.ShapeDtypeStruct((PEER_HEADS, t), F32)],
        compiler_params=_params(("parallel",)),
    )(q2d, keys)


def _peer_dense_kernel(x_ref, u_ref, vt_ref, s_ref, e_ref, thr_ref, o_ref, act_ref, *, te):
    j = pl.program_id(1)

    @pl.when(j == 0)
    def _():
        o_ref[...] = jnp.zeros(o_ref.shape, F32)

    a = lax.dot_general(u_ref[...], x_ref[...], _NT, preferred_element_type=F32)
    ge = 0.5 * a * (1.0 + lax.erf(a * (1.0 / math.sqrt(2.0))))
    per = te // PEER_NKEYS
    for ii in range(per):
        i1 = j * per + ii
        gate = jnp.zeros((PEER_NKEYS, x_ref.shape[0]), F32)
        for h in range(PEER_HEADS):
            s1 = s_ref[2 * h, pl.ds(i1, 1), :]
            e1 = e_ref[2 * h, pl.ds(i1, 1), :]
            hit = (s1 + s_ref[2 * h + 1]) >= thr_ref[h:h + 1, :]
            gate = gate + jnp.where(hit, e_ref[2 * h + 1], 0.0) * e1
        rows = slice(ii * PEER_NKEYS, (ii + 1) * PEER_NKEYS)
        act_ref[rows, :] = (ge[rows, :] * gate).astype(BF16)
    o_ref[...] += jnp.dot(vt_ref[...], act_ref[...], preferred_element_type=F32)


def _peer_dense(h2d, u_bf, vt_bf, s_t, e_t, thr, tt=512, te=512):
    t, d = h2d.shape
    ne = u_bf.shape[0]
    nhp = 2 * PEER_HEADS
    big = pl.BlockSpec((nhp, PEER_NKEYS, tt), lambda i, j: (0, 0, i))
    return pl.pallas_call(
        functools.partial(_peer_dense_kernel, te=te),
        name="peer_dense",
        grid=(t // tt, ne // te),
        in_specs=[pl.BlockSpec((tt, d), lambda i, j: (i, 0)),
                  pl.BlockSpec((te, d), lambda i, j: (j, 0)),
                  pl.BlockSpec((d, te), lambda i, j: (0, j)),
                  big, big, pl.BlockSpec((PEER_HEADS, tt), lambda i, j: (0, i))],
        out_specs=pl.BlockSpec((d, tt), lambda i, j: (0, i)),
        out_shape=jax.ShapeDtypeStruct((d, t), F32),
        scratch_shapes=[pltpu.VMEM((te, tt), BF16)],
        compiler_params=_params(("parallel", "arbitrary"), vmem_mb=56),
    )(h2d, u_bf, vt_bf, s_t, e_t, thr)


def _res_t_kernel(x_ref, yt_ref, g_ref, o_ref):
    o_ref[0] = x_ref[0] + g_ref[0] * yt_ref[...].T


def _residual_t(x, y_t, modr, row_gate, tm=512, tn=512):
    b, s, d = x.shape
    nt = s // tm
    return pl.pallas_call(
        _res_t_kernel,
        name="residual_t",
        grid=(b, nt, d // tn),
        in_specs=[pl.BlockSpec((1, tm, tn), lambda bi, i, j: (bi, i, j)),
                  pl.BlockSpec((tn, tm), lambda bi, i, j: (j, bi * nt + i)),
                  pl.BlockSpec((1, 1, tn), lambda bi, i, j: (row_gate(bi), 0, j))],
        out_specs=pl.BlockSpec((1, tm, tn), lambda bi, i, j: (bi, i, j)),
        out_shape=jax.ShapeDtypeStruct((b, s, d), F32),
        compiler_params=_params(("parallel", "parallel", "parallel")),
    )(x, y_t, modr)


def _pad_cols(a, width):
    return jnp.pad(a, ((0, 0), (0, width - a.shape[1])))


def _mla_weights(w_in, w_uq, g_qn, g_kn):
    w_in_p = _pad_cols(w_in, MLA_Q_RANK + MLA_KV_RANK + LANES).astype(BF16)
    w_uq_p = jnp.pad(w_uq.reshape(MLA_Q_RANK, MLA_HEADS, MLA_QK),
                     ((0, 0), (0, 0), (0, MLA_HEAD_PAD - MLA_QK)))
    w_uq_p = w_uq_p.reshape(MLA_Q_RANK, MLA_HEADS * MLA_HEAD_PAD).astype(BF16)
    pad = lambda g: _pad_cols(g.reshape(1, MLA_QK), MLA_HEAD_PAD)
    return w_in_p, w_uq_p, pad(g_qn), pad(g_kn)


def _peer(x, layer, norm_row, modr, row, w_q, sub_keys, u_tab, v_tab):
    b, s, d = x.shape
    q = _norm_matmul(x, norm_row, modr, row(layer, 4), row(layer, 3), w_q.astype(BF16),
                     tm=512, tn=512, out_dtype=F32)
    hmod = _norm_mod(x, norm_row, modr, row(layer, 4), row(layer, 3))
    s_t, e_t, thr = _peer_topk(q.reshape(b * s, -1), sub_keys)
    y_t = _peer_dense(hmod.reshape(b * s, d), u_tab.astype(BF16), v_tab.T.astype(BF16), s_t, e_t, thr)
    return _residual_t(x, y_t, modr, row(layer, 5))


def kernel(x, c, positions, ada_w, ada_b, norm_g, mla_w_in, mla_g_q, mla_w_uq, mla_g_kv, mla_w_ukv, mla_g_qn, mla_g_kn, mla_w_o, dil_w_in, dil_g_qn, dil_g_kn, dil_w_o, peer_w_q, peer_sub_keys, peer_u, peer_v):
    b, s, d = x.shape
    depth = ada_w.shape[0]
    mod = _ada_mod(c, ada_w, ada_b)
    modr = mod.reshape(depth * b * 6, 1, d)
    row = lambda layer, part: (lambda bi: (layer * b + bi) * 6 + part)
    m_cos, m_sin_lo, m_sin_hi, d_cos, d_sin = _rope_tables(positions)

    for layer in range(depth):
        g1 = norm_g[layer, 0].reshape(1, d)
        g2 = norm_g[layer, 1].reshape(1, d)
        a = layer // 2
        if layer % 2 == 0:
            w_in_p, w_uq_p, g_qn_p, g_kn_p = _mla_weights(mla_w_in[a], mla_w_uq[a], mla_g_qn[a], mla_g_kn[a])
            z = _norm_matmul(x, g1, modr, row(layer, 1), row(layer, 0), w_in_p,
                             tm=512, tn=w_in_p.shape[1], out_dtype=F32)
            q, k, v = _mla_prep(z, mla_g_q[a].reshape(1, -1), mla_g_kv[a].reshape(1, -1), w_uq_p,
                                mla_w_ukv[a].astype(BF16), g_qn_p, g_kn_p, (m_cos, m_sin_lo, m_sin_hi))
            o = _flash_attention(q, k, v)
            x = _proj_residual(o, mla_w_o[a].astype(BF16), x, modr, row(layer, 2))
        else:
            hmod = _norm_mod(x, g1, modr, row(layer, 1), row(layer, 0))
            w_in = dil_w_in[a].astype(BF16)
            ones = jnp.ones((len(DIL_GROUPS), DIL_HEAD_DIM), F32)
            gains = jnp.stack([dil_g_qn[a], dil_g_kn[a], ones], axis=1).reshape(-1, 1, DIL_HEAD_DIM)
            state = None
            for gi, (_, dil) in enumerate(DIL_GROUPS):
                qkv = _dil_project(hmod, w_in, gains, d_cos, d_sin, gi, dil)
                state = _dil_attention(qkv, state, is_last=gi == len(DIL_GROUPS) - 1)
            x = _proj_residual(state, dil_w_o[a].astype(BF16), x, modr, row(layer, 2))
        x = _peer(x, layer, g2, modr, row, peer_w_q[layer], peer_sub_keys[layer], peer_u[layer], peer_v[layer])
    return x
```

```python
import functools
import math

import jax
import jax.numpy as jnp
from jax import lax
from jax.experimental import pallas as pl
from jax.experimental.pallas import tpu as pltpu

F32 = jnp.float32
BF16 = jnp.bfloat16
EPS = 1e-6
ROPE_THETA = 10000.0
LOG2E = 1.4426950408889634
NEG_INF = float("-inf")
LANES = 128

MLA_HEADS = 16
MLA_Q_RANK = 512
MLA_KV_RANK = 512
MLA_NOPE = 128
MLA_ROPE = 64
MLA_QK = MLA_NOPE + MLA_ROPE
MLA_V = 128
MLA_HEAD_PAD = 256

DIL_GROUPS = ((128, 1), (512, 4), (2048, 16))
DIL_HEADS = 16
DIL_HEAD_DIM = 128
DIL_STEPS = 128

PEER_HEADS = 8
PEER_NKEYS = 128
PEER_TOPK = 16

_NT = (((1,), (1,)), ((), ()))


def _params(sem, vmem_mb=48):
    return pltpu.CompilerParams(dimension_semantics=sem, vmem_limit_bytes=vmem_mb << 20)


def _ada_kernel(c_ref, w_ref, b_ref, o_ref):
    c = c_ref[...]
    sc = c / (1.0 + jnp.exp(-c))
    o_ref[0] = jnp.dot(sc, w_ref[0], preferred_element_type=F32,
                       precision=lax.Precision.HIGHEST) + b_ref[0]


def _ada_mod(c, ada_w, ada_b, tn=768):
    depth, d, n = ada_w.shape
    b = c.shape[0]
    return pl.pallas_call(
        _ada_kernel,
        name="ada_mod",
        grid=(depth, n // tn),
        in_specs=[pl.BlockSpec((b, d), lambda l, j: (0, 0)),
                  pl.BlockSpec((1, d, tn), lambda l, j: (l, 0, j)),
                  pl.BlockSpec((1, 1, tn), lambda l, j: (l, 0, j))],
        out_specs=pl.BlockSpec((1, b, tn), lambda l, j: (l, 0, j)),
        out_shape=jax.ShapeDtypeStruct((depth, b, n), F32),
        compiler_params=_params(("parallel", "parallel")),
    )(c, ada_w, ada_b.reshape(depth, 1, n))


def _rope_tab_kernel(pos_ref, fm_ref, fd_ref, mc_ref, ms1_ref, ms2_ref, dc_ref, ds_ref):
    pos = pos_ref[0].astype(F32)
    lane = lax.broadcasted_iota(jnp.int32, (pos.shape[0], LANES), 1)
    am = pos * fm_ref[...]
    cm, sm = jnp.cos(am), jnp.sin(am)
    mc_ref[0] = jnp.where(lane < 2 * (MLA_ROPE // 2), cm, 0.0)
    ms1_ref[0] = jnp.where(lane < MLA_ROPE // 2, -sm, 0.0)
    ms2_ref[0] = jnp.where(lane < MLA_ROPE // 2, 0.0, jnp.where(lane < MLA_ROPE, sm, 0.0))
    ad = pos * fd_ref[...]
    dc_ref[0] = jnp.cos(ad)
    sd = jnp.sin(ad)
    ds_ref[0] = jnp.where(lane < DIL_HEAD_DIM // 2, -sd, sd)


def _rope_tables(positions, ts=512):
    b, s = positions.shape
    hm, hd = MLA_ROPE // 2, DIL_HEAD_DIM // 2
    inv_m = ROPE_THETA ** (-jnp.arange(hm, dtype=F32) / hm)
    inv_d = ROPE_THETA ** (-jnp.arange(hd, dtype=F32) / hd)
    fm = jnp.concatenate([inv_m, inv_m, jnp.zeros((LANES - 2 * hm,), F32)]).reshape(1, LANES)
    fd = jnp.concatenate([inv_d, inv_d]).reshape(1, LANES)
    tab = jax.ShapeDtypeStruct((b, s, LANES), F32)
    row = pl.BlockSpec((1, ts, LANES), lambda bi, i: (bi, i, 0))
    frq = pl.BlockSpec((1, LANES), lambda bi, i: (0, 0))
    return pl.pallas_call(
        _rope_tab_kernel,
        name="rope_tables",
        grid=(b, s // ts),
        in_specs=[pl.BlockSpec((1, ts, 1), lambda bi, i: (bi, i, 0)), frq, frq],
        out_specs=[row] * 5,
        out_shape=[tab] * 5,
        compiler_params=_params(("parallel", "parallel")),
    )(positions.reshape(b, s, 1), fm, fd)


def _norm_matmul_kernel(x_ref, g_ref, sc_ref, sh_ref, w_ref, o_ref, h_ref):
    @pl.when(pl.program_id(2) == 0)
    def _():
        x = x_ref[0]
        inv = lax.rsqrt(jnp.mean(x * x, axis=-1, keepdims=True) + EPS)
        h = (x * inv) * g_ref[...]
        h_ref[...] = (h * (1.0 + sc_ref[0]) + sh_ref[0]).astype(BF16)

    o_ref[0] = jnp.dot(h_ref[...], w_ref[...], preferred_element_type=F32).astype(o_ref.dtype)


def _norm_mod_kernel(x_ref, g_ref, sc_ref, sh_ref, o_ref):
    x = x_ref[0]
    inv = lax.rsqrt(jnp.mean(x * x, axis=-1, keepdims=True) + EPS)
    h = (x * inv) * g_ref[...]
    o_ref[0] = (h * (1.0 + sc_ref[0]) + sh_ref[0]).astype(o_ref.dtype)


def _norm_mod(x, g_row, modr, row_scale, row_shift, tm=512):
    b, s, d = x.shape
    return pl.pallas_call(
        _norm_mod_kernel,
        name="norm_mod",
        grid=(b, s // tm),
        in_specs=[pl.BlockSpec((1, tm, d), lambda bi, i: (bi, i, 0)),
                  pl.BlockSpec((1, d), lambda bi, i: (0, 0)),
                  pl.BlockSpec((1, 1, d), lambda bi, i: (row_scale(bi), 0, 0)),
                  pl.BlockSpec((1, 1, d), lambda bi, i: (row_shift(bi), 0, 0))],
        out_specs=pl.BlockSpec((1, tm, d), lambda bi, i: (bi, i, 0)),
        out_shape=jax.ShapeDtypeStruct((b, s, d), BF16),
        compiler_params=_params(("parallel", "parallel")),
    )(x, g_row, modr, modr)


def _norm_matmul(x, g_row, modr, row_scale, row_shift, w, tm, tn, out_dtype):
    b, s, d = x.shape
    n = w.shape[1]
    return pl.pallas_call(
        _norm_matmul_kernel,
        name="norm_matmul",
        grid=(b, s // tm, n // tn),
        in_specs=[pl.BlockSpec((1, tm, d), lambda bi, i, j: (bi, i, 0)),
                  pl.BlockSpec((1, d), lambda bi, i, j: (0, 0)),
                  pl.BlockSpec((1, 1, d), lambda bi, i, j: (row_scale(bi), 0, 0)),
                  pl.BlockSpec((1, 1, d), lambda bi, i, j: (row_shift(bi), 0, 0)),
                  pl.BlockSpec((d, tn), lambda bi, i, j: (0, j))],
        out_specs=pl.BlockSpec((1, tm, tn), lambda bi, i, j: (bi, i, j)),
        out_shape=jax.ShapeDtypeStruct((b, s, n), out_dtype),
        scratch_shapes=[pltpu.VMEM((tm, d), BF16)],
        compiler_params=_params(("parallel", "parallel", "arbitrary")),
    )(x, g_row, modr, modr, w)


def _proj_res_kernel(a_ref, w_ref, x_ref, g_ref, o_ref):
    y = jnp.dot(a_ref[0], w_ref[...], preferred_element_type=F32)
    o_ref[0] = x_ref[0] + g_ref[0] * y


def _proj_residual(a, w, x, modr, row_gate, tm=512, tn=512):
    b, s, k = a.shape
    d = w.shape[1]
    return pl.pallas_call(
        _proj_res_kernel,
        name="proj_residual",
        grid=(b, s // tm, d // tn),
        in_specs=[pl.BlockSpec((1, tm, k), lambda bi, i, j: (bi, i, 0)),
                  pl.BlockSpec((k, tn), lambda bi, i, j: (0, j)),
                  pl.BlockSpec((1, tm, tn), lambda bi, i, j: (bi, i, j)),
                  pl.BlockSpec((1, 1, tn), lambda bi, i, j: (row_gate(bi), 0, j))],
        out_specs=pl.BlockSpec((1, tm, tn), lambda bi, i, j: (bi, i, j)),
        out_shape=jax.ShapeDtypeStruct((b, s, d), F32),
        compiler_params=_params(("parallel", "parallel", "parallel")),
    )(a, w, x, modr)


def _mla_prep_kernel(z_ref, gq_ref, gkv_ref, wuq_ref, wukv_ref, gqn_ref, gkn_ref,
                     c_ref, s1_ref, s2_ref, q_ref, k_ref, vt_ref, *, q_scale):
    z = z_ref[0]
    cq = z[:, :MLA_Q_RANK]
    ckv = z[:, MLA_Q_RANK:MLA_Q_RANK + MLA_KV_RANK]
    kr = z[:, MLA_Q_RANK + MLA_KV_RANK:]

    def rms(t, g):
        return (t * lax.rsqrt(jnp.mean(t * t, axis=-1, keepdims=True) + EPS)) * g

    qa = jnp.dot(rms(cq, gq_ref[...]).astype(BF16), wuq_ref[...], preferred_element_type=F32)
    kva = jnp.dot(rms(ckv, gkv_ref[...]).astype(BF16), wukv_ref[...], preferred_element_type=F32)
    cos, sin_lo, sin_hi = c_ref[0], s1_ref[0], s2_ref[0]

    def rope(t):
        return (t * cos + pltpu.roll(t, LANES - MLA_ROPE // 2, 1) * sin_lo
                + pltpu.roll(t, MLA_ROPE // 2, 1) * sin_hi)

    gqn, gkn = gqn_ref[...], gkn_ref[...]
    kr_ss = jnp.sum(kr * kr, axis=-1, keepdims=True)
    kr_rot = rope(kr * gkn[:, MLA_NOPE:])
    for h in range(MLA_HEADS):
        lo = h * MLA_HEAD_PAD
        qh = qa[:, lo:lo + MLA_HEAD_PAD]
        inv = lax.rsqrt(jnp.sum(qh * qh, axis=-1, keepdims=True) * (1.0 / MLA_QK) + EPS) * q_scale
        q_ref[0, h, :, :MLA_NOPE] = ((qh[:, :MLA_NOPE] * gqn[:, :MLA_NOPE]) * inv).astype(BF16)
        q_ref[0, h, :, MLA_NOPE:] = (rope(qh[:, MLA_NOPE:] * gqn[:, MLA_NOPE:]) * inv).astype(BF16)
        kn = kva[:, lo:lo + MLA_NOPE]
        kss = jnp.sum(kn * kn, axis=-1, keepdims=True) + kr_ss
        kinv = lax.rsqrt(kss * (1.0 / MLA_QK) + EPS)
        k_ref[0, h, :, :MLA_NOPE] = ((kn * gkn[:, :MLA_NOPE]) * kinv).astype(BF16)
        k_ref[0, h, :, MLA_NOPE:] = (kr_rot * kinv).astype(BF16)
        vt_ref[0, h] = kva[:, lo + MLA_NOPE:lo + MLA_HEAD_PAD].T.astype(BF16)


def _mla_prep(z, g_q, g_kv, w_uq_p, w_ukv, g_qn_p, g_kn_p, tabs, tm=256):
    b, s, zw = z.shape
    nh = MLA_HEADS
    row = lambda w: pl.BlockSpec((1, w), lambda bi, i: (0, 0))
    full = lambda a: pl.BlockSpec(a.shape, lambda bi, i: (0, 0))
    tab = pl.BlockSpec((1, tm, LANES), lambda bi, i: (bi, i, 0))
    head = lambda w: pl.BlockSpec((1, nh, tm, w), lambda bi, i: (bi, 0, i, 0))
    kern = functools.partial(_mla_prep_kernel, q_scale=MLA_QK ** -0.5 * LOG2E)
    return pl.pallas_call(
        kern,
        name="mla_prep",
        grid=(b, s // tm),
        in_specs=[pl.BlockSpec((1, tm, zw), lambda bi, i: (bi, i, 0)),
                  row(MLA_Q_RANK), row(MLA_KV_RANK), full(w_uq_p), full(w_ukv),
                  row(MLA_HEAD_PAD), row(MLA_HEAD_PAD), tab, tab, tab],
        out_specs=[head(MLA_HEAD_PAD), head(MLA_HEAD_PAD),
                   pl.BlockSpec((1, nh, MLA_V, tm), lambda bi, i: (bi, 0, 0, i))],
        out_shape=[jax.ShapeDtypeStruct((b, nh, s, MLA_HEAD_PAD), BF16),
                   jax.ShapeDtypeStruct((b, nh, s, MLA_HEAD_PAD), BF16),
                   jax.ShapeDtypeStruct((b, nh, MLA_V, s), BF16)],
        compiler_params=_params(("parallel", "parallel")),
    )(z, g_q, g_kv, w_uq_p, w_ukv, g_qn_p, g_kn_p, *tabs)


def _flash_kernel(q_ref, k_ref, vt_ref, o_ref, acc_ref, *, t, nsub):
    qi = pl.program_id(2)
    acc_ref[...] = jnp.zeros(acc_ref.shape, F32)
    qs = [q_ref[0, 0, c * t:(c + 1) * t, :] for c in range(nsub)]

    def scores(c, j, diagonal):
        start = pl.multiple_of(j * t, t)
        k = k_ref[0, 0, pl.ds(start, t), :]
        st = lax.dot_general(k, qs[c], _NT, preferred_element_type=F32)
        if diagonal:
            key = lax.broadcasted_iota(jnp.int32, (t, t), 0)
            qry = lax.broadcasted_iota(jnp.int32, (t, t), 1)
            st = jnp.where(key <= qry, st, NEG_INF)
        return st

    def soft(st, m, l):
        m_new = jnp.maximum(m, jnp.max(st, axis=0, keepdims=True))
        p = jnp.exp2(st - m_new)
        alpha = jnp.exp2(m - m_new)
        l_new = alpha * l + jnp.sum(p, axis=0, keepdims=True)
        return p.astype(BF16), alpha, m_new, l_new

    def accum(c, j, p, alpha):
        start = pl.multiple_of(j * t, t)
        vt = vt_ref[0, 0, :, pl.ds(start, t)]
        acc_ref[c] = alpha * acc_ref[c] + jnp.dot(vt, p, preferred_element_type=F32)

    def multi(chains, j, carry, diag_chain):
        carry = list(carry)
        sts = [scores(c, j, c == diag_chain) for c in chains]
        ps = []
        for c, st in zip(chains, sts):
            p, alpha, carry[2 * c], carry[2 * c + 1] = soft(st, carry[2 * c], carry[2 * c + 1])
            ps.append((p, alpha))
        for c, (p, alpha) in zip(chains, ps):
            accum(c, j, p, alpha)
        return tuple(carry)

    def body(j, carry):
        return multi(range(nsub), j, carry, -1)

    init = []
    for c in range(nsub):
        init += [jnp.full((1, t), NEG_INF, F32), jnp.zeros((1, t), F32)]
    carry = lax.fori_loop(0, nsub * qi, body, tuple(init))
    for d in range(nsub):
        carry = multi(range(d, nsub), nsub * qi + d, carry, d)
    for c in range(nsub):
        ot = acc_ref[c] / carry[2 * c + 1]
        o_ref[0, c * t:(c + 1) * t, :] = ot.T.astype(o_ref.dtype)


def _flash_attention(q, k, vt, t=512, nsub=4):
    b, nh, s, dq = q.shape
    dv = vt.shape[2]
    tq = t * nsub
    return pl.pallas_call(
        functools.partial(_flash_kernel, t=t, nsub=nsub),
        name="mla_flash",
        grid=(b, nh, s // tq),
        in_specs=[pl.BlockSpec((1, 1, tq, dq), lambda bi, h, i: (bi, h, i, 0)),
                  pl.BlockSpec((1, 1, s, dq), lambda bi, h, i: (bi, h, 0, 0)),
                  pl.BlockSpec((1, 1, dv, s), lambda bi, h, i: (bi, h, 0, 0))],
        out_specs=pl.BlockSpec((1, tq, dv), lambda bi, h, i: (bi, i, h)),
        out_shape=jax.ShapeDtypeStruct((b, s, nh * dv), BF16),
        scratch_shapes=[pltpu.VMEM((nsub, dv, t), F32)],
        compiler_params=_params(("parallel", "parallel", "arbitrary")),
    )(q, k, vt)


_DIL_TN = 512
_DIL_HPT = _DIL_TN // DIL_HEAD_DIM
_DIL_TPP = DIL_HEADS // _DIL_HPT


def _dil_proj_kernel(h_ref, w_ref, gain_ref, c_ref, s_ref, o_ref, *, q_scale):
    part = pl.program_id(3) // _DIL_TPP
    y = jnp.dot(h_ref[0], w_ref[...], preferred_element_type=F32)

    @pl.when(part == 2)
    def _():
        for hh in range(_DIL_HPT):
            o_ref[0, 0, 0, hh] = y[:, hh * LANES:(hh + 1) * LANES].astype(BF16)

    @pl.when(part < 2)
    def _():
        cos, sin = c_ref[0], s_ref[0]
        gain = gain_ref[0]
        post = jnp.where(part == 0, q_scale, 1.0)
        for hh in range(_DIL_HPT):
            yh = y[:, hh * LANES:(hh + 1) * LANES]
            inv = lax.rsqrt(jnp.mean(yh * yh, axis=-1, keepdims=True) + EPS)
            yn = (yh * inv) * gain
            rot = yn * cos + pltpu.roll(yn, DIL_HEAD_DIM // 2, 1) * sin
            o_ref[0, 0, 0, hh] = (rot * post).astype(BF16)


def _dil_project(h, w_in, gains, cos_t, sin_t, group, dil):
    b, s, d = h.shape
    l = s // dil
    tl = min(512, l)
    tiles = 3 * _DIL_TPP
    kern = functools.partial(_dil_proj_kernel, q_scale=DIL_HEAD_DIM ** -0.5 * LOG2E)
    return pl.pallas_call(
        kern,
        name=f"dil_proj_d{dil}",
        grid=(b, dil, l // tl, tiles),
        in_specs=[pl.BlockSpec((1, tl, d), lambda bi, r, i, j: (bi, i, r)),
                  pl.BlockSpec((d, _DIL_TN), lambda bi, r, i, j: (0, group * tiles + j)),
                  pl.BlockSpec((1, 1, LANES), lambda bi, r, i, j: (group * 3 + j // _DIL_TPP, 0, 0)),
                  pl.BlockSpec((1, tl, LANES), lambda bi, r, i, j: (bi, i, r)),
                  pl.BlockSpec((1, tl, LANES), lambda bi, r, i, j: (bi, i, r))],
        out_specs=pl.BlockSpec((1, 1, 1, _DIL_HPT, tl, LANES),
                               lambda bi, r, i, j: (bi, r, j // _DIL_TPP, j % _DIL_TPP, i, 0)),
        out_shape=jax.ShapeDtypeStruct((b, dil, 3, DIL_HEADS, l, LANES), BF16),
        compiler_params=_params(("parallel", "parallel", "parallel", "arbitrary")),
    )(h.reshape(b, l, dil * d), w_in, gains,
      cos_t.reshape(b, l, dil * LANES), sin_t.reshape(b, l, dil * LANES))


def _dil_attn_kernel(*refs, tq, has_state, is_last):
    q_ref, kc_ref, kp_ref, vc_ref, vp_ref = refs[:5]
    refs = refs[5:]
    if has_state:
        acc_in, m_in, l_in = refs[:3]
        refs = refs[3:]
    if is_last:
        (o_ref,) = refs
    else:
        acc_out, m_out, l_out = refs

    first = pl.program_id(2) == 0
    row = lax.broadcasted_iota(jnp.int32, (tq, tq), 0)
    col = lax.broadcasted_iota(jnp.int32, (tq, tq), 1)
    dist = row - col
    bias_c = jnp.where(dist >= 0, jnp.where(dist <= DIL_STEPS, 0.0, NEG_INF), NEG_INF)
    rowp = lax.broadcasted_iota(jnp.int32, (tq, DIL_STEPS), 0)
    colp = lax.broadcasted_iota(jnp.int32, (tq, DIL_STEPS), 1)
    bias_p = jnp.where(colp >= rowp, jnp.where(first, NEG_INF, 0.0), NEG_INF)
    lane = lax.broadcasted_iota(jnp.int32, (tq, LANES), 1)
    if has_state:
        m_old, l_old = m_in[0], l_in[0]
    m_tile = jnp.zeros((tq, LANES), F32)
    l_tile = jnp.zeros((tq, LANES), F32)

    for h in range(DIL_HEADS):
        q = q_ref[0, 0, 0, h]
        sc = lax.dot_general(q, kc_ref[0, 0, 0, h], _NT, preferred_element_type=F32) + bias_c
        sp = lax.dot_general(q, kp_ref[0, 0, 0, h], _NT, preferred_element_type=F32) + bias_p
        m_new = jnp.maximum(jnp.max(sc, axis=-1, keepdims=True), jnp.max(sp, axis=-1, keepdims=True))
        if has_state:
            m_prev = m_old[:, h:h + 1]
            m_new = jnp.maximum(m_new, m_prev)
            alpha = jnp.exp2(m_prev - m_new)
        pc = jnp.exp2(sc - m_new)
        pp = jnp.exp2(sp - m_new)
        l_new = jnp.sum(pc, axis=-1, keepdims=True) + jnp.sum(pp, axis=-1, keepdims=True)
        acc = (jnp.dot(pc.astype(BF16), vc_ref[0, 0, 0, h], preferred_element_type=F32)
               + jnp.dot(pp.astype(BF16), vp_ref[0, 0, 0, h], preferred_element_type=F32))
        cols = slice(h * LANES, (h + 1) * LANES)
        if has_state:
            l_new = l_new + alpha * l_old[:, h:h + 1]
            acc = acc + alpha * acc_in[0, :, cols]
        if is_last:
            o_ref[0, :, cols] = (acc / l_new).astype(o_ref.dtype)
        else:
            acc_out[0, :, cols] = acc
            m_tile = jnp.where(lane == h, m_new, m_tile)
            l_tile = jnp.where(lane == h, l_new, l_tile)
    if not is_last:
        m_out[0] = m_tile
        l_out[0] = l_tile


def _dil_attention(qkv, state, is_last, tq=256):
    b, dil, _, nh, l, dh = qkv.shape
    tq = min(tq, l)
    width = nh * dh
    ratio = tq // DIL_STEPS
    cur = lambda part: pl.BlockSpec((1, 1, 1, nh, tq, dh), lambda bi, r, i: (bi, r, part, 0, i, 0))
    prev = lambda part: pl.BlockSpec(
        (1, 1, 1, nh, DIL_STEPS, dh),
        lambda bi, r, i: (bi, r, part, 0, jnp.maximum(i * ratio - 1, 0), 0))
    wide = pl.BlockSpec((1, tq, width), lambda bi, r, i: (bi, i, r))
    thin = pl.BlockSpec((1, tq, LANES), lambda bi, r, i: (bi, i, r))
    in_specs = [cur(0), cur(1), prev(1), cur(2), prev(2)]
    args = [qkv] * 5
    if state is not None:
        in_specs += [wide, thin, thin]
        args += [state[0].reshape(b, l, dil * width), state[1].reshape(b, l, dil * LANES),
                 state[2].reshape(b, l, dil * LANES)]
    if is_last:
        out_specs = wide
        out_shape = jax.ShapeDtypeStruct((b, l, dil * width), BF16)
    else:
        out_specs = [wide, thin, thin]
        out_shape = [jax.ShapeDtypeStruct((b, l, dil * width), F32),
                     jax.ShapeDtypeStruct((b, l, dil * LANES), F32),
                     jax.ShapeDtypeStruct((b, l, dil * LANES), F32)]
    kern = functools.partial(_dil_attn_kernel, tq=tq, has_state=state is not None, is_last=is_last)
    out = pl.pallas_call(
        kern,
        name=f"dil_attn_d{dil}",
        grid=(b, dil, l // tq),
        in_specs=in_specs,
        out_specs=out_specs,
        out_shape=out_shape,
        compiler_params=_params(("parallel", "parallel", "parallel")),
    )(*args)
    s = l * dil
    if is_last:
        return out.reshape(b, s, width)
    return (out[0].reshape(b, s, width), out[1].reshape(b, s, LANES), out[2].reshape(b, s, LANES))


def _take_top(cur, count):
    nrows = cur.shape[0]
    rows = lax.broadcasted_iota(jnp.int32, cur.shape, 0)
    rank = jnp.full(cur.shape, float(count), F32)
    vals = []
    for r in range(count):
        mx = jnp.max(cur, axis=0, keepdims=True)
        vals.append(mx)
        hit = rows == jnp.min(jnp.where(cur == mx, rows, nrows), axis=0, keepdims=True)
        cur = jnp.where(hit, NEG_INF, cur)
        rank = jnp.where(hit, float(r), rank)
    return vals, rank


def _peer_topk_kernel(q_ref, keys_ref, rk2_ref, c1_ref, e1_ref, e2_ref, s_ref):
    k = PEER_TOPK
    tt = q_ref.shape[0]
    sub = lax.broadcasted_iota(jnp.int32, (8, tt), 0)

    def count(slab, thr):
        return jnp.sum(jnp.where(slab >= thr, 1.0, 0.0), axis=0, keepdims=True)

    for h in range(PEER_HEADS):
        tops = []
        for p in range(2):
            hp = 2 * h + p
            qs = q_ref[:, hp * PEER_NKEYS:(hp + 1) * PEER_NKEYS]
            st = lax.dot_general(keys_ref[hp], qs, _NT, preferred_element_type=F32,
                                 precision=lax.Precision.HIGHEST)
            s_ref[p] = st
            tops.append(_take_top(st, k))
        (a, rank1), (bv, rank2) = tops
        b_lo = jnp.concatenate(bv[:8], axis=0)
        b_hi = jnp.concatenate(bv[8:], axis=0)
        a_hi = jnp.concatenate(a[8:], axis=0)
        slabs = [a[0] + b_lo, a[0] + b_hi, a[1] + b_lo]
        for i in range(2, 8):
            slabs.append(jnp.where(sub < k // (i + 1), a[i] + b_lo, NEG_INF))
        slabs.append(a_hi + bv[0])
        best, _ = _take_top(jnp.concatenate(slabs, axis=0), k)
        top, thr = best[0], best[k - 1]
        z = jnp.sum(jnp.exp(jnp.concatenate(best, axis=0) - top), axis=0, keepdims=True)
        cnt = [count(slabs[0], thr) + count(slabs[1], thr)]
        cnt += [count(slabs[i + 1], thr) for i in range(1, 8)]
        tail = jnp.where(slabs[9] >= thr, 1.0, 0.0)
        cnt += [tail[i:i + 1] for i in range(8)]
        c1 = jnp.zeros(rank1.shape, F32)
        for r in range(k):
            c1 = jnp.where(rank1 == float(r), cnt[r], c1)
        c1_ref[h] = c1
        rk2_ref[h] = rank2.astype(BF16)
        e1_ref[h] = jnp.exp(s_ref[0] - a[0]) / z
        e2_ref[h] = jnp.exp(s_ref[1] - bv[0]).astype(BF16)


def _peer_topk(q2d, sub_keys, tt=512):
    t = q2d.shape[0]
    nh = PEER_HEADS
    keys = sub_keys.reshape(2 * nh, PEER_NKEYS, sub_keys.shape[-1])
    big = pl.BlockSpec((nh, PEER_NKEYS, tt), lambda i: (0, 0, i))
    return pl.pallas_call(
        _peer_topk_kernel,
        name="peer_topk",
        grid=(t // tt,),
        in_specs=[pl.BlockSpec((tt, q2d.shape[1]), lambda i: (i, 0)),
                  pl.BlockSpec(keys.shape, lambda i: (0, 0, 0))],
        out_specs=[big, big, big, big],
        out_shape=[jax.ShapeDtypeStruct((nh, PEER_NKEYS, t), BF16),
                   jax.ShapeDtypeStruct((nh, PEER_NKEYS, t), F32),
                   jax.ShapeDtypeStruct((nh, PEER_NKEYS, t), F32),
                   jax.ShapeDtypeStruct((nh, PEER_NKEYS, t), BF16)],
        scratch_shapes=[pltpu.VMEM((2, PEER_NKEYS, tt), F32)],
        compiler_params=_params(("parallel",)),
    )(q2d, keys)


def _peer_dense_kernel(x_ref, u_ref, vt_ref, rk2_ref, c1_ref, e1_ref, e2_ref, o_ref, act_ref, *, te):
    j = pl.program_id(1)

    @pl.when(j == 0)
    def _():
        o_ref[...] = jnp.zeros(o_ref.shape, F32)

    half = te // 2
    per = half // PEER_NKEYS
    pre = [lax.dot_general(u_ref[sb * half:(sb + 1) * half, :], x_ref[...], _NT,
                           preferred_element_type=F32) for sb in range(2)]
    for sb in range(2):
        a = pre[sb]
        ge = (0.5 * a * (1.0 + lax.erf(a * (1.0 / math.sqrt(2.0))))).astype(BF16)
        for ii in range(per):
            i1 = j * (2 * per) + sb * per + ii
            gate = jnp.zeros((PEER_NKEYS, x_ref.shape[0]), BF16)
            for h in range(PEER_HEADS):
                c1 = c1_ref[h, pl.ds(i1, 1), :].astype(BF16)
                e1 = e1_ref[h, pl.ds(i1, 1), :].astype(BF16)
                gate = gate + jnp.where(rk2_ref[h] < c1, e2_ref[h], jnp.zeros((), BF16)) * e1
            lo = sb * half + ii * PEER_NKEYS
            act_ref[lo:lo + PEER_NKEYS, :] = ge[ii * PEER_NKEYS:(ii + 1) * PEER_NKEYS, :] * gate
    o_ref[...] += jnp.dot(vt_ref[...], act_ref[...], preferred_element_type=F32)


def _peer_dense(h2d, u_bf, vt_bf, rk2, c1, e1, e2, tt=512, te=1024):
    t, d = h2d.shape
    ne = u_bf.shape[0]
    big = pl.BlockSpec((PEER_HEADS, PEER_NKEYS, tt), lambda i, j: (0, 0, i))
    return pl.pallas_call(
        functools.partial(_peer_dense_kernel, te=te),
        name="peer_dense",
        grid=(t // tt, ne // te),
        in_specs=[pl.BlockSpec((tt, d), lambda i, j: (i, 0)),
                  pl.BlockSpec((te, d), lambda i, j: (j, 0)),
                  pl.BlockSpec((d, te), lambda i, j: (0, j)),
                  big, big, big, big],
        out_specs=pl.BlockSpec((d, tt), lambda i, j: (0, i)),
        out_shape=jax.ShapeDtypeStruct((d, t), F32),
        scratch_shapes=[pltpu.VMEM((te, tt), BF16)],
        compiler_params=_params(("parallel", "arbitrary"), vmem_mb=56),
    )(h2d, u_bf, vt_bf, rk2, c1, e1, e2)


def _res_t_kernel(x_ref, yt_ref, g_ref, o_ref):
    o_ref[0] = x_ref[0] + g_ref[0] * yt_ref[...].T


def _residual_t(x, y_t, modr, row_gate, tm=512, tn=512):
    b, s, d = x.shape
    nt = s // tm
    return pl.pallas_call(
        _res_t_kernel,
        name="residual_t",
        grid=(b, nt, d // tn),
        in_specs=[pl.BlockSpec((1, tm, tn), lambda bi, i, j: (bi, i, j)),
                  pl.BlockSpec((tn, tm), lambda bi, i, j: (j, bi * nt + i)),
                  pl.BlockSpec((1, 1, tn), lambda bi, i, j: (row_gate(bi), 0, j))],
        out_specs=pl.BlockSpec((1, tm, tn), lambda bi, i, j: (bi, i, j)),
        out_shape=jax.ShapeDtypeStruct((b, s, d), F32),
        compiler_params=_params(("parallel", "parallel", "parallel")),
    )(x, y_t, modr)


def _pad_cols(a, width):
    return jnp.pad(a, ((0, 0), (0, width - a.shape[1])))


def _mla_weights(w_in, w_uq, g_qn, g_kn):
    w_in_p = _pad_cols(w_in, MLA_Q_RANK + MLA_KV_RANK + LANES).astype(BF16)
    w_uq_p = jnp.pad(w_uq.reshape(MLA_Q_RANK, MLA_HEADS, MLA_QK),
                     ((0, 0), (0, 0), (0, MLA_HEAD_PAD - MLA_QK)))
    w_uq_p = w_uq_p.reshape(MLA_Q_RANK, MLA_HEADS * MLA_HEAD_PAD).astype(BF16)
    pad = lambda g: _pad_cols(g.reshape(1, MLA_QK), MLA_HEAD_PAD)
    return w_in_p, w_uq_p, pad(g_qn), pad(g_kn)


def _peer(x, layer, norm_row, modr, row, w_q, sub_keys, u_tab, v_tab):
    b, s, d = x.shape
    q = _norm_matmul(x, norm_row, modr, row(layer, 4), row(layer, 3), w_q.astype(BF16),
                     tm=512, tn=512, out_dtype=F32)
    hmod = _norm_mod(x, norm_row, modr, row(layer, 4), row(layer, 3))
    rk2, c1, e1, e2 = _peer_topk(q.reshape(b * s, -1), sub_keys)
    y_t = _peer_dense(hmod.reshape(b * s, d), u_tab.astype(BF16), v_tab.T.astype(BF16), rk2, c1, e1, e2)
    return _residual_t(x, y_t, modr, row(layer, 5))


def kernel(x, c, positions, ada_w, ada_b, norm_g, mla_w_in, mla_g_q, mla_w_uq, mla_g_kv, mla_w_ukv, mla_g_qn, mla_g_kn, mla_w_o, dil_w_in, dil_g_qn, dil_g_kn, dil_w_o, peer_w_q, peer_sub_keys, peer_u, peer_v):
    b, s, d = x.shape
    depth = ada_w.shape[0]
    mod = _ada_mod(c, ada_w, ada_b)
    modr = mod.reshape(depth * b * 6, 1, d)
    row = lambda layer, part: (lambda bi: (layer * b + bi) * 6 + part)
    m_cos, m_sin_lo, m_sin_hi, d_cos, d_sin = _rope_tables(positions)

    for layer in range(depth):
        g1 = norm_g[layer, 0].reshape(1, d)
        g2 = norm_g[layer, 1].reshape(1, d)
        a = layer // 2
        if layer % 2 == 0:
            w_in_p, w_uq_p, g_qn_p, g_kn_p = _mla_weights(mla_w_in[a], mla_w_uq[a], mla_g_qn[a], mla_g_kn[a])
            z = _norm_matmul(x, g1, modr, row(layer, 1), row(layer, 0), w_in_p,
                             tm=512, tn=w_in_p.shape[1], out_dtype=F32)
            q, k, v = _mla_prep(z, mla_g_q[a].reshape(1, -1), mla_g_kv[a].reshape(1, -1), w_uq_p,
                                mla_w_ukv[a].astype(BF16), g_qn_p, g_kn_p, (m_cos, m_sin_lo, m_sin_hi))
            o = _flash_attention(q, k, v)
            x = _proj_residual(o, mla_w_o[a].astype(BF16), x, modr, row(layer, 2))
        else:
            hmod = _norm_mod(x, g1, modr, row(layer, 1), row(layer, 0))
            w_in = dil_w_in[a].astype(BF16)
            ones = jnp.ones((len(DIL_GROUPS), DIL_HEAD_DIM), F32)
            gains = jnp.stack([dil_g_qn[a], dil_g_kn[a], ones], axis=1).reshape(-1, 1, DIL_HEAD_DIM)
            state = None
            for gi, (_, dil) in enumerate(DIL_GROUPS):
                qkv = _dil_project(hmod, w_in, gains, d_cos, d_sin, gi, dil)
                state = _dil_attention(qkv, state, is_last=gi == len(DIL_GROUPS) - 1)
            x = _proj_residual(state, dil_w_o[a].astype(BF16), x, modr, row(layer, 2))
        x = _peer(x, layer, g2, modr, row, peer_w_q[layer], peer_sub_keys[layer], peer_u[layer], peer_v[layer])
    return x
```

```python
import functools
import math

import jax
import jax.numpy as jnp
from jax import lax
from jax.experimental import pallas as pl
from jax.experimental.pallas import tpu as pltpu

F32 = jnp.float32
BF16 = jnp.bfloat16
EPS = 1e-6
ROPE_THETA = 10000.0
LOG2E = 1.4426950408889634
NEG_INF = float("-inf")
LANES = 128

MLA_HEADS = 16
MLA_Q_RANK = 512
MLA_KV_RANK = 512
MLA_NOPE = 128
MLA_ROPE = 64
MLA_QK = MLA_NOPE + MLA_ROPE
MLA_V = 128
MLA_HEAD_PAD = 256

DIL_GROUPS = ((128, 1), (512, 4), (2048, 16))
DIL_HEADS = 16
DIL_HEAD_DIM = 128
DIL_STEPS = 128

PEER_HEADS = 8
PEER_NKEYS = 128
PEER_TOPK = 16

_NT = (((1,), (1,)), ((), ()))


def _params(sem, vmem_mb=48):
    return pltpu.CompilerParams(dimension_semantics=sem, vmem_limit_bytes=vmem_mb << 20)


def _ada_kernel(c_ref, w_ref, b_ref, o_ref):
    c = c_ref[...]
    sc = c / (1.0 + jnp.exp(-c))
    o_ref[0] = jnp.dot(sc, w_ref[0], preferred_element_type=F32,
                       precision=lax.Precision.HIGHEST) + b_ref[0]


def _ada_mod(c, ada_w, ada_b, tn=768):
    depth, d, n = ada_w.shape
    b = c.shape[0]
    return pl.pallas_call(
        _ada_kernel,
        name="ada_mod",
        grid=(depth, n // tn),
        in_specs=[pl.BlockSpec((b, d), lambda l, j: (0, 0)),
                  pl.BlockSpec((1, d, tn), lambda l, j: (l, 0, j)),
                  pl.BlockSpec((1, 1, tn), lambda l, j: (l, 0, j))],
        out_specs=pl.BlockSpec((1, b, tn), lambda l, j: (l, 0, j)),
        out_shape=jax.ShapeDtypeStruct((depth, b, n), F32),
        compiler_params=_params(("parallel", "parallel")),
    )(c, ada_w, ada_b.reshape(depth, 1, n))


def _rope_tab_kernel(pos_ref, fm_ref, fd_ref, mc_ref, ms1_ref, ms2_ref, dc_ref, ds_ref):
    pos = pos_ref[0].astype(F32)
    lane = lax.broadcasted_iota(jnp.int32, (pos.shape[0], LANES), 1)
    am = pos * fm_ref[...]
    cm, sm = jnp.cos(am), jnp.sin(am)
    mc_ref[0] = jnp.where(lane < 2 * (MLA_ROPE // 2), cm, 0.0)
    ms1_ref[0] = jnp.where(lane < MLA_ROPE // 2, -sm, 0.0)
    ms2_ref[0] = jnp.where(lane < MLA_ROPE // 2, 0.0, jnp.where(lane < MLA_ROPE, sm, 0.0))
    ad = pos * fd_ref[...]
    dc_ref[0] = jnp.cos(ad)
    sd = jnp.sin(ad)
    ds_ref[0] = jnp.where(lane < DIL_HEAD_DIM // 2, -sd, sd)


def _rope_tables(positions, ts=512):
    b, s = positions.shape
    hm, hd = MLA_ROPE // 2, DIL_HEAD_DIM // 2
    inv_m = ROPE_THETA ** (-jnp.arange(hm, dtype=F32) / hm)
    inv_d = ROPE_THETA ** (-jnp.arange(hd, dtype=F32) / hd)
    fm = jnp.concatenate([inv_m, inv_m, jnp.zeros((LANES - 2 * hm,), F32)]).reshape(1, LANES)
    fd = jnp.concatenate([inv_d, inv_d]).reshape(1, LANES)
    tab = jax.ShapeDtypeStruct((b, s, LANES), F32)
    row = pl.BlockSpec((1, ts, LANES), lambda bi, i: (bi, i, 0))
    frq = pl.BlockSpec((1, LANES), lambda bi, i: (0, 0))
    return pl.pallas_call(
        _rope_tab_kernel,
        name="rope_tables",
        grid=(b, s // ts),
        in_specs=[pl.BlockSpec((1, ts, 1), lambda bi, i: (bi, i, 0)), frq, frq],
        out_specs=[row] * 5,
        out_shape=[tab] * 5,
        compiler_params=_params(("parallel", "parallel")),
    )(positions.reshape(b, s, 1), fm, fd)


def _norm_matmul_kernel(x_ref, g_ref, sc_ref, sh_ref, w_ref, o_ref, *h_out):
    x = x_ref[0]
    inv = lax.rsqrt(jnp.mean(x * x, axis=-1, keepdims=True) + EPS)
    h = (x * inv) * g_ref[...]
    h = (h * (1.0 + sc_ref[0]) + sh_ref[0]).astype(BF16)
    for h_ref in h_out:
        h_ref[0] = h
    o_ref[0] = jnp.dot(h, w_ref[...], preferred_element_type=F32)


def _norm_mod_kernel(x_ref, g_ref, sc_ref, sh_ref, o_ref):
    x = x_ref[0]
    inv = lax.rsqrt(jnp.mean(x * x, axis=-1, keepdims=True) + EPS)
    h = (x * inv) * g_ref[...]
    o_ref[0] = (h * (1.0 + sc_ref[0]) + sh_ref[0]).astype(o_ref.dtype)


def _norm_mod(x, g_row, modr, row_scale, row_shift, tm=512):
    b, s, d = x.shape
    return pl.pallas_call(
        _norm_mod_kernel,
        name="norm_mod",
        grid=(b, s // tm),
        in_specs=[pl.BlockSpec((1, tm, d), lambda bi, i: (bi, i, 0)),
                  pl.BlockSpec((1, d), lambda bi, i: (0, 0)),
                  pl.BlockSpec((1, 1, d), lambda bi, i: (row_scale(bi), 0, 0)),
                  pl.BlockSpec((1, 1, d), lambda bi, i: (row_shift(bi), 0, 0))],
        out_specs=pl.BlockSpec((1, tm, d), lambda bi, i: (bi, i, 0)),
        out_shape=jax.ShapeDtypeStruct((b, s, d), BF16),
        compiler_params=_params(("parallel", "parallel")),
    )(x, g_row, modr, modr)


def _norm_matmul(x, g_row, modr, row_scale, row_shift, w, emit_h, tm=512):
    b, s, d = x.shape
    n = w.shape[1]
    rows = lambda width: pl.BlockSpec((1, tm, width), lambda bi, i: (bi, i, 0))
    out_specs, out_shape = [rows(n)], [jax.ShapeDtypeStruct((b, s, n), F32)]
    if emit_h:
        out_specs.append(rows(d))
        out_shape.append(jax.ShapeDtypeStruct((b, s, d), BF16))
    return pl.pallas_call(
        _norm_matmul_kernel,
        name="norm_matmul",
        grid=(b, s // tm),
        in_specs=[rows(d),
                  pl.BlockSpec((1, d), lambda bi, i: (0, 0)),
                  pl.BlockSpec((1, 1, d), lambda bi, i: (row_scale(bi), 0, 0)),
                  pl.BlockSpec((1, 1, d), lambda bi, i: (row_shift(bi), 0, 0)),
                  pl.BlockSpec((d, n), lambda bi, i: (0, 0))],
        out_specs=out_specs,
        out_shape=out_shape,
        compiler_params=_params(("parallel", "parallel")),
    )(x, g_row, modr, modr, w)


def _proj_res_kernel(a_ref, w_ref, x_ref, g_ref, o_ref):
    y = jnp.dot(a_ref[0], w_ref[...], preferred_element_type=F32)
    o_ref[0] = x_ref[0] + g_ref[0] * y


def _proj_residual(a, w, x, modr, row_gate, tm=512):
    b, s, k = a.shape
    d = w.shape[1]
    tn = d
    return pl.pallas_call(
        _proj_res_kernel,
        name="proj_residual",
        grid=(b, s // tm, d // tn),
        in_specs=[pl.BlockSpec((1, tm, k), lambda bi, i, j: (bi, i, 0)),
                  pl.BlockSpec((k, tn), lambda bi, i, j: (0, j)),
                  pl.BlockSpec((1, tm, tn), lambda bi, i, j: (bi, i, j)),
                  pl.BlockSpec((1, 1, tn), lambda bi, i, j: (row_gate(bi), 0, j))],
        out_specs=pl.BlockSpec((1, tm, tn), lambda bi, i, j: (bi, i, j)),
        out_shape=jax.ShapeDtypeStruct((b, s, d), F32),
        compiler_params=_params(("parallel", "parallel", "parallel")),
    )(a, w, x, modr)


def _mla_prep_kernel(z_ref, gq_ref, gkv_ref, wuq_ref, wukv_ref, gqn_ref, gkn_ref,
                     c_ref, s1_ref, s2_ref, q_ref, k_ref, vt_ref, *, q_scale):
    z = z_ref[0]
    cq = z[:, :MLA_Q_RANK]
    ckv = z[:, MLA_Q_RANK:MLA_Q_RANK + MLA_KV_RANK]
    kr = z[:, MLA_Q_RANK + MLA_KV_RANK:]

    def rms(t, g):
        return (t * lax.rsqrt(jnp.mean(t * t, axis=-1, keepdims=True) + EPS)) * g

    qa = jnp.dot(rms(cq, gq_ref[...]).astype(BF16), wuq_ref[...], preferred_element_type=F32)
    kva = jnp.dot(rms(ckv, gkv_ref[...]).astype(BF16), wukv_ref[...], preferred_element_type=F32)
    cos, sin_lo, sin_hi = c_ref[0], s1_ref[0], s2_ref[0]

    def rope(t):
        return (t * cos + pltpu.roll(t, LANES - MLA_ROPE // 2, 1) * sin_lo
                + pltpu.roll(t, MLA_ROPE // 2, 1) * sin_hi)

    gqn, gkn = gqn_ref[...], gkn_ref[...]
    kr_ss = jnp.sum(kr * kr, axis=-1, keepdims=True)
    kr_rot = rope(kr * gkn[:, MLA_NOPE:])
    for h in range(MLA_HEADS):
        lo = h * MLA_HEAD_PAD
        qh = qa[:, lo:lo + MLA_HEAD_PAD]
        inv = lax.rsqrt(jnp.sum(qh * qh, axis=-1, keepdims=True) * (1.0 / MLA_QK) + EPS) * q_scale
        q_ref[0, h, :, :MLA_NOPE] = ((qh[:, :MLA_NOPE] * gqn[:, :MLA_NOPE]) * inv).astype(BF16)
        q_ref[0, h, :, MLA_NOPE:] = (rope(qh[:, MLA_NOPE:] * gqn[:, MLA_NOPE:]) * inv).astype(BF16)
        kn = kva[:, lo:lo + MLA_NOPE]
        kss = jnp.sum(kn * kn, axis=-1, keepdims=True) + kr_ss
        kinv = lax.rsqrt(kss * (1.0 / MLA_QK) + EPS)
        k_ref[0, h, :, :MLA_NOPE] = ((kn * gkn[:, :MLA_NOPE]) * kinv).astype(BF16)
        k_ref[0, h, :, MLA_NOPE:] = (kr_rot * kinv).astype(BF16)
        vt_ref[0, h] = kva[:, lo + MLA_NOPE:lo + MLA_HEAD_PAD].T.astype(BF16)


def _mla_prep(z, g_q, g_kv, w_uq_p, w_ukv, g_qn_p, g_kn_p, tabs, tm=256):
    b, s, zw = z.shape
    nh = MLA_HEADS
    row = lambda w: pl.BlockSpec((1, w), lambda bi, i: (0, 0))
    full = lambda a: pl.BlockSpec(a.shape, lambda bi, i: (0, 0))
    tab = pl.BlockSpec((1, tm, LANES), lambda bi, i: (bi, i, 0))
    head = lambda w: pl.BlockSpec((1, nh, tm, w), lambda bi, i: (bi, 0, i, 0))
    kern = functools.partial(_mla_prep_kernel, q_scale=MLA_QK ** -0.5 * LOG2E)
    return pl.pallas_call(
        kern,
        name="mla_prep",
        grid=(b, s // tm),
        in_specs=[pl.BlockSpec((1, tm, zw), lambda bi, i: (bi, i, 0)),
                  row(MLA_Q_RANK), row(MLA_KV_RANK), full(w_uq_p), full(w_ukv),
                  row(MLA_HEAD_PAD), row(MLA_HEAD_PAD), tab, tab, tab],
        out_specs=[head(MLA_HEAD_PAD), head(MLA_HEAD_PAD),
                   pl.BlockSpec((1, nh, MLA_V, tm), lambda bi, i: (bi, 0, 0, i))],
        out_shape=[jax.ShapeDtypeStruct((b, nh, s, MLA_HEAD_PAD), BF16),
                   jax.ShapeDtypeStruct((b, nh, s, MLA_HEAD_PAD), BF16),
                   jax.ShapeDtypeStruct((b, nh, MLA_V, s), BF16)],
        compiler_params=_params(("parallel", "parallel")),
    )(z, g_q, g_kv, w_uq_p, w_ukv, g_qn_p, g_kn_p, *tabs)


def _flash_kernel(q_ref, k_ref, vt_ref, o_ref, acc_ref, *, t, nsub):
    qi = pl.program_id(2)
    acc_ref[...] = jnp.zeros(acc_ref.shape, F32)
    qs = [q_ref[0, 0, c * t:(c + 1) * t, :] for c in range(nsub)]

    def scores(c, j, diagonal):
        start = pl.multiple_of(j * t, t)
        k = k_ref[0, 0, pl.ds(start, t), :]
        st = lax.dot_general(k, qs[c], _NT, preferred_element_type=F32)
        if diagonal:
            key = lax.broadcasted_iota(jnp.int32, (t, t), 0)
            qry = lax.broadcasted_iota(jnp.int32, (t, t), 1)
            st = jnp.where(key <= qry, st, NEG_INF)
        return st

    def soft(st, m, l):
        m_new = jnp.maximum(m, jnp.max(st, axis=0, keepdims=True))
        p = jnp.exp2(st - m_new)
        alpha = jnp.exp2(m - m_new)
        l_new = alpha * l + jnp.sum(p, axis=0, keepdims=True)
        return p.astype(BF16), alpha, m_new, l_new

    def accum(c, j, p, alpha):
        start = pl.multiple_of(j * t, t)
        vt = vt_ref[0, 0, :, pl.ds(start, t)]
        acc_ref[c] = alpha * acc_ref[c] + jnp.dot(vt, p, preferred_element_type=F32)

    def multi(chains, j, carry, diag_chain):
        carry = list(carry)
        sts = [scores(c, j, c == diag_chain) for c in chains]
        ps = []
        for c, st in zip(chains, sts):
            p, alpha, carry[2 * c], carry[2 * c + 1] = soft(st, carry[2 * c], carry[2 * c + 1])
            ps.append((p, alpha))
        for c, (p, alpha) in zip(chains, ps):
            accum(c, j, p, alpha)
        return tuple(carry)

    def body(j, carry):
        return multi(range(nsub), j, carry, -1)

    init = []
    for c in range(nsub):
        init += [jnp.full((1, t), NEG_INF, F32), jnp.zeros((1, t), F32)]
    carry = lax.fori_loop(0, nsub * qi, body, tuple(init))
    for d in range(nsub):
        carry = multi(range(d, nsub), nsub * qi + d, carry, d)
    for c in range(nsub):
        ot = acc_ref[c] / carry[2 * c + 1]
        o_ref[0, c * t:(c + 1) * t, :] = ot.T.astype(o_ref.dtype)


def _flash_attention(q, k, vt, t=512, nsub=4):
    b, nh, s, dq = q.shape
    dv = vt.shape[2]
    tq = t * nsub
    return pl.pallas_call(
        functools.partial(_flash_kernel, t=t, nsub=nsub),
        name="mla_flash",
        grid=(b, nh, s // tq),
        in_specs=[pl.BlockSpec((1, 1, tq, dq), lambda bi, h, i: (bi, h, i, 0)),
                  pl.BlockSpec((1, 1, s, dq), lambda bi, h, i: (bi, h, 0, 0)),
                  pl.BlockSpec((1, 1, dv, s), lambda bi, h, i: (bi, h, 0, 0))],
        out_specs=pl.BlockSpec((1, tq, dv), lambda bi, h, i: (bi, i, h)),
        out_shape=jax.ShapeDtypeStruct((b, s, nh * dv), BF16),
        scratch_shapes=[pltpu.VMEM((nsub, dv, t), F32)],
        compiler_params=_params(("parallel", "parallel", "arbitrary")),
    )(q, k, vt)


_DIL_TN = 1024
_DIL_SUB = 256
_DIL_HPT = _DIL_TN // DIL_HEAD_DIM
_DIL_TPP = DIL_HEADS // _DIL_HPT


def _dil_proj_kernel(h_ref, w_ref, gain_ref, c_ref, s_ref, o_ref, *, q_scale):
    part = pl.program_id(3) // _DIL_TPP
    nsub = _DIL_TN // _DIL_SUB
    hps = _DIL_SUB // LANES
    h = h_ref[0]
    ys = [jnp.dot(h, w_ref[:, sb * _DIL_SUB:(sb + 1) * _DIL_SUB], preferred_element_type=F32)
          for sb in range(nsub)]
    is_v = part == 2
    cos, sin = c_ref[0], s_ref[0]
    gain = gain_ref[0]
    post = jnp.where(part == 0, q_scale, 1.0)
    for sb in range(nsub):
        for hh in range(hps):
            yh = ys[sb][:, hh * LANES:(hh + 1) * LANES]
            inv = lax.rsqrt(jnp.mean(yh * yh, axis=-1, keepdims=True) + EPS)
            yn = (yh * inv) * gain
            rot = yn * cos + pltpu.roll(yn, DIL_HEAD_DIM // 2, 1) * sin
            o_ref[0, 0, 0, sb * hps + hh] = jnp.where(is_v, yh, rot * post).astype(BF16)


def _dil_project(h, w_in, gains, cos_t, sin_t, group, dil):
    b, s, d = h.shape
    l = s // dil
    tl = min(512, l)
    tiles = 3 * _DIL_TPP
    kern = functools.partial(_dil_proj_kernel, q_scale=DIL_HEAD_DIM ** -0.5 * LOG2E)
    return pl.pallas_call(
        kern,
        name=f"dil_proj_d{dil}",
        grid=(b, dil, l // tl, tiles),
        in_specs=[pl.BlockSpec((1, tl, d), lambda bi, r, i, j: (bi, i, r)),
                  pl.BlockSpec((d, _DIL_TN), lambda bi, r, i, j: (0, group * tiles + j)),
                  pl.BlockSpec((1, 1, LANES), lambda bi, r, i, j: (group * 3 + j // _DIL_TPP, 0, 0)),
                  pl.BlockSpec((1, tl, LANES), lambda bi, r, i, j: (bi, i, r)),
                  pl.BlockSpec((1, tl, LANES), lambda bi, r, i, j: (bi, i, r))],
        out_specs=pl.BlockSpec((1, 1, 1, _DIL_HPT, tl, LANES),
                               lambda bi, r, i, j: (bi, r, j // _DIL_TPP, j % _DIL_TPP, i, 0)),
        out_shape=jax.ShapeDtypeStruct((b, dil, 3, DIL_HEADS, l, LANES), BF16),
        compiler_params=_params(("parallel", "parallel", "parallel", "arbitrary")),
    )(h.reshape(b, l, dil * d), w_in, gains,
      cos_t.reshape(b, l, dil * LANES), sin_t.reshape(b, l, dil * LANES))


def _dil_attn_kernel(*refs, tq, has_state, is_last):
    q_ref, kc_ref, kp_ref, vc_ref, vp_ref = refs[:5]
    refs = refs[5:]
    if has_state:
        acc_in, m_in, l_in = refs[:3]
        refs = refs[3:]
    if is_last:
        (o_ref,) = refs
    else:
        acc_out, m_out, l_out = refs

    first = pl.program_id(2) == 0
    row = lax.broadcasted_iota(jnp.int32, (tq, tq), 0)
    col = lax.broadcasted_iota(jnp.int32, (tq, tq), 1)
    dist = row - col
    bias_c = jnp.where(dist >= 0, jnp.where(dist <= DIL_STEPS, 0.0, NEG_INF), NEG_INF)
    rowp = lax.broadcasted_iota(jnp.int32, (tq, DIL_STEPS), 0)
    colp = lax.broadcasted_iota(jnp.int32, (tq, DIL_STEPS), 1)
    bias_p = jnp.where(colp >= rowp, jnp.where(first, NEG_INF, 0.0), NEG_INF)
    lane = lax.broadcasted_iota(jnp.int32, (tq, LANES), 1)
    if has_state:
        m_old, l_old = m_in[0], l_in[0]
    m_tile = jnp.zeros((tq, LANES), F32)
    l_tile = jnp.zeros((tq, LANES), F32)

    for h in range(DIL_HEADS):
        q = q_ref[0, 0, 0, h]
        sc = lax.dot_general(q, kc_ref[0, 0, 0, h], _NT, preferred_element_type=F32) + bias_c
        sp = lax.dot_general(q, kp_ref[0, 0, 0, h], _NT, preferred_element_type=F32) + bias_p
        m_new = jnp.maximum(jnp.max(sc, axis=-1, keepdims=True), jnp.max(sp, axis=-1, keepdims=True))
        if has_state:
            m_prev = m_old[:, h:h + 1]
            m_new = jnp.maximum(m_new, m_prev)
            alpha = jnp.exp2(m_prev - m_new)
        pc = jnp.exp2(sc - m_new)
        pp = jnp.exp2(sp - m_new)
        l_new = jnp.sum(pc, axis=-1, keepdims=True) + jnp.sum(pp, axis=-1, keepdims=True)
        acc = (jnp.dot(pc.astype(BF16), vc_ref[0, 0, 0, h], preferred_element_type=F32)
               + jnp.dot(pp.astype(BF16), vp_ref[0, 0, 0, h], preferred_element_type=F32))
        cols = slice(h * LANES, (h + 1) * LANES)
        if has_state:
            l_new = l_new + alpha * l_old[:, h:h + 1]
            acc = acc + alpha * acc_in[0, :, cols]
        if is_last:
            o_ref[0, :, cols] = (acc / l_new).astype(o_ref.dtype)
        else:
            acc_out[0, :, cols] = acc
            m_tile = jnp.where(lane == h, m_new, m_tile)
            l_tile = jnp.where(lane == h, l_new, l_tile)
    if not is_last:
        m_out[0] = m_tile
        l_out[0] = l_tile


def _dil_attention(qkv, state, is_last, tq=256):
    b, dil, _, nh, l, dh = qkv.shape
    tq = min(tq, l)
    width = nh * dh
    ratio = tq // DIL_STEPS
    cur = lambda part: pl.BlockSpec((1, 1, 1, nh, tq, dh), lambda bi, r, i: (bi, r, part, 0, i, 0))
    prev = lambda part: pl.BlockSpec(
        (1, 1, 1, nh, DIL_STEPS, dh),
        lambda bi, r, i: (bi, r, part, 0, jnp.maximum(i * ratio - 1, 0), 0))
    wide = pl.BlockSpec((1, tq, width), lambda bi, r, i: (bi, i, r))
    thin = pl.BlockSpec((1, tq, LANES), lambda bi, r, i: (bi, i, r))
    in_specs = [cur(0), cur(1), prev(1), cur(2), prev(2)]
    args = [qkv] * 5
    if state is not None:
        in_specs += [wide, thin, thin]
        args += [state[0].reshape(b, l, dil * width), state[1].reshape(b, l, dil * LANES),
                 state[2].reshape(b, l, dil * LANES)]
    if is_last:
        out_specs = wide
        out_shape = jax.ShapeDtypeStruct((b, l, dil * width), BF16)
    else:
        out_specs = [wide, thin, thin]
        out_shape = [jax.ShapeDtypeStruct((b, l, dil * width), F32),
                     jax.ShapeDtypeStruct((b, l, dil * LANES), F32),
                     jax.ShapeDtypeStruct((b, l, dil * LANES), F32)]
    kern = functools.partial(_dil_attn_kernel, tq=tq, has_state=state is not None, is_last=is_last)
    out = pl.pallas_call(
        kern,
        name=f"dil_attn_d{dil}",
        grid=(b, dil, l // tq),
        in_specs=in_specs,
        out_specs=out_specs,
        out_shape=out_shape,
        compiler_params=_params(("parallel", "parallel", "parallel")),
    )(*args)
    s = l * dil
    if is_last:
        return out.reshape(b, s, width)
    return (out[0].reshape(b, s, width), out[1].reshape(b, s, LANES), out[2].reshape(b, s, LANES))


def _batcher_pairs(n):
    pairs = []
    p = 1
    while p < n:
        k = p
        while k >= 1:
            for j in range(k % p, n - k, 2 * k):
                for i in range(min(k, n - j - k)):
                    if (i + j) // (2 * p) == (i + j + k) // (2 * p):
                        pairs.append((i + j, i + j + k))
            k //= 2
        p *= 2
    return pairs


def _sorted_top(vs, count):
    vs = list(vs) + [None] * (count - len(vs))

    def exchange(i, j):
        a, b = vs[i], vs[j]
        if b is None:
            return
        if a is None:
            vs[i], vs[j] = b, None
            return
        vs[i], vs[j] = jnp.maximum(a, b), jnp.minimum(a, b)

    for i, j in _batcher_pairs(count):
        exchange(i, j)
    shape = next(v.shape for v in vs if v is not None)
    vs = [jnp.full(shape, NEG_INF, F32) if v is None else v for v in vs]
    for shift in (4, 2, 1):
        other = [pltpu.roll(v, shift, 0) for v in vs]
        vs = [jnp.maximum(vs[i], other[count - 1 - i]) for i in range(count)]
        k = count // 2
        while k >= 1:
            for i in range(count):
                if i & k == 0:
                    exchange(i, i + k)
            k //= 2
    return vs


def _peer_topk_kernel(q_ref, keys_ref, rk2_ref, c1_ref, e1_ref, e2_ref):
    k = PEER_TOPK
    tt = q_ref.shape[0]
    nslab = PEER_NKEYS // 8
    sub = lax.broadcasted_iota(jnp.int32, (8, tt), 0)

    def by_sublane(vals):
        out = vals[0]
        for i in range(1, 8):
            out = jnp.where(sub == i, vals[i], out)
        return out

    def count(slab, thr):
        return jnp.sum(jnp.where(slab >= thr, 1.0, 0.0), axis=0, keepdims=True)

    for h in range(PEER_HEADS):
        sts, tops = [], []
        for p in range(2):
            hp = 2 * h + p
            qs = q_ref[:, hp * PEER_NKEYS:(hp + 1) * PEER_NKEYS]
            st = lax.dot_general(keys_ref[hp], qs, _NT, preferred_element_type=F32,
                                 precision=lax.Precision.HIGHEST)
            slabs = [st[8 * i:8 * i + 8, :] for i in range(nslab)]
            sts.append(slabs)
            tops.append(_sorted_top(slabs, k))
        a, bv = tops
        b_lo, b_hi, a_hi = by_sublane(bv[:8]), by_sublane(bv[8:]), by_sublane(a[8:])
        cand = [a[0] + b_lo, a[0] + b_hi, a[1] + b_lo]
        for i in range(2, 8):
            cand.append(jnp.where(sub < k // (i + 1), a[i] + b_lo, NEG_INF))
        cand.append(a_hi + bv[0])
        best = _sorted_top(cand, k)
        top, thr = best[0], best[k - 1]
        z = jnp.exp(best[0] - top)
        for r in range(1, k):
            z = z + jnp.exp(best[r] - top)
        cnt = [count(cand[0], thr) + count(cand[1], thr)]
        cnt += [count(cand[i + 1], thr) for i in range(1, 8)]
        tail = jnp.where(cand[9] >= thr, 1.0, 0.0)
        cnt += [tail[i:i + 1] for i in range(8)]
        inv_z = 1.0 / z
        for i in range(nslab):
            s1, s2 = sts[0][i], sts[1][i]
            c1 = jnp.zeros(s1.shape, F32)
            rk = jnp.full(s2.shape, float(k), F32)
            for r in range(k - 1, -1, -1):
                c1 = jnp.where(s1 >= a[r], cnt[r], c1)
                rk = jnp.where(s2 >= bv[r], float(r), rk)
            rows = slice(8 * i, 8 * i + 8)
            c1_ref[h, rows, :] = c1
            rk2_ref[h, rows, :] = rk.astype(BF16)
            e1_ref[h, rows, :] = jnp.exp(s1 - a[0]) * inv_z
            e2_ref[h, rows, :] = jnp.exp(s2 - bv[0]).astype(BF16)


def _peer_topk(q2d, sub_keys, tt=512):
    t = q2d.shape[0]
    nh = PEER_HEADS
    keys = sub_keys.reshape(2 * nh, PEER_NKEYS, sub_keys.shape[-1])
    big = pl.BlockSpec((nh, PEER_NKEYS, tt), lambda i: (0, 0, i))
    return pl.pallas_call(
        _peer_topk_kernel,
        name="peer_topk",
        grid=(t // tt,),
        in_specs=[pl.BlockSpec((tt, q2d.shape[1]), lambda i: (i, 0)),
                  pl.BlockSpec(keys.shape, lambda i: (0, 0, 0))],
        out_specs=[big, big, big, big],
        out_shape=[jax.ShapeDtypeStruct((nh, PEER_NKEYS, t), BF16),
                   jax.ShapeDtypeStruct((nh, PEER_NKEYS, t), F32),
                   jax.ShapeDtypeStruct((nh, PEER_NKEYS, t), F32),
                   jax.ShapeDtypeStruct((nh, PEER_NKEYS, t), BF16)],
        compiler_params=_params(("parallel",)),
    )(q2d, keys)


def _peer_dense_kernel(x_ref, u_ref, vt_ref, rk2_ref, c1_ref, e1_ref, e2_ref, o_ref, act_ref, *, te):
    j = pl.program_id(1)

    @pl.when(j == 0)
    def _():
        o_ref[...] = jnp.zeros(o_ref.shape, F32)

    half = te // 2
    per = half // PEER_NKEYS
    pre = [lax.dot_general(u_ref[sb * half:(sb + 1) * half, :], x_ref[...], _NT,
                           preferred_element_type=F32) for sb in range(2)]
    for sb in range(2):
        a = pre[sb]
        ge = (0.5 * a * (1.0 + lax.erf(a * (1.0 / math.sqrt(2.0))))).astype(BF16)
        for ii in range(per):
            i1 = j * (2 * per) + sb * per + ii
            gate = jnp.zeros((PEER_NKEYS, x_ref.shape[0]), BF16)
            for h in range(PEER_HEADS):
                c1 = c1_ref[h, pl.ds(i1, 1), :].astype(BF16)
                e1 = e1_ref[h, pl.ds(i1, 1), :].astype(BF16)
                gate = gate + jnp.where(rk2_ref[h] < c1, e2_ref[h], jnp.zeros((), BF16)) * e1
            lo = sb * half + ii * PEER_NKEYS
            act_ref[lo:lo + PEER_NKEYS, :] = ge[ii * PEER_NKEYS:(ii + 1) * PEER_NKEYS, :] * gate
    o_ref[...] += jnp.dot(vt_ref[...], act_ref[...], preferred_element_type=F32)


def _peer_dense(h2d, u_bf, vt_bf, rk2, c1, e1, e2, tt=512, te=1024):
    t, d = h2d.shape
    ne = u_bf.shape[0]
    big = pl.BlockSpec((PEER_HEADS, PEER_NKEYS, tt), lambda i, j: (0, 0, i))
    return pl.pallas_call(
        functools.partial(_peer_dense_kernel, te=te),
        name="peer_dense",
        grid=(t // tt, ne // te),
        in_specs=[pl.BlockSpec((tt, d), lambda i, j: (i, 0)),
                  pl.BlockSpec((te, d), lambda i, j: (j, 0)),
                  pl.BlockSpec((d, te), lambda i, j: (0, j)),
                  big, big, big, big],
        out_specs=pl.BlockSpec((d, tt), lambda i, j: (0, i)),
        out_shape=jax.ShapeDtypeStruct((d, t), F32),
        scratch_shapes=[pltpu.VMEM((te, tt), BF16)],
        compiler_params=_params(("parallel", "arbitrary"), vmem_mb=56),
    )(h2d, u_bf, vt_bf, rk2, c1, e1, e2)


def _res_t_kernel(x_ref, yt_ref, g_ref, o_ref):
    o_ref[0] = x_ref[0] + g_ref[0] * yt_ref[...].T


def _residual_t(x, y_t, modr, row_gate, tm=512, tn=512):
    b, s, d = x.shape
    nt = s // tm
    return pl.pallas_call(
        _res_t_kernel,
        name="residual_t",
        grid=(b, nt, d // tn),
        in_specs=[pl.BlockSpec((1, tm, tn), lambda bi, i, j: (bi, i, j)),
                  pl.BlockSpec((tn, tm), lambda bi, i, j: (j, bi * nt + i)),
                  pl.BlockSpec((1, 1, tn), lambda bi, i, j: (row_gate(bi), 0, j))],
        out_specs=pl.BlockSpec((1, tm, tn), lambda bi, i, j: (bi, i, j)),
        out_shape=jax.ShapeDtypeStruct((b, s, d), F32),
        compiler_params=_params(("parallel", "parallel", "parallel")),
    )(x, y_t, modr)


def _pad_cols(a, width):
    return jnp.pad(a, ((0, 0), (0, width - a.shape[1])))


def _mla_weights(w_in, w_uq, g_qn, g_kn):
    w_in_p = _pad_cols(w_in, MLA_Q_RANK + MLA_KV_RANK + LANES).astype(BF16)
    w_uq_p = jnp.pad(w_uq.reshape(MLA_Q_RANK, MLA_HEADS, MLA_QK),
                     ((0, 0), (0, 0), (0, MLA_HEAD_PAD - MLA_QK)))
    w_uq_p = w_uq_p.reshape(MLA_Q_RANK, MLA_HEADS * MLA_HEAD_PAD).astype(BF16)
    pad = lambda g: _pad_cols(g.reshape(1, MLA_QK), MLA_HEAD_PAD)
    return w_in_p, w_uq_p, pad(g_qn), pad(g_kn)


def _peer(x, layer, norm_row, modr, row, w_q, sub_keys, u_tab, v_tab):
    b, s, d = x.shape
    q, hmod = _norm_matmul(x, norm_row, modr, row(layer, 4), row(layer, 3), w_q.astype(BF16), emit_h=True)
    rk2, c1, e1, e2 = _peer_topk(q.reshape(b * s, -1), sub_keys)
    y_t = _peer_dense(hmod.reshape(b * s, d), u_tab.astype(BF16), v_tab.T.astype(BF16), rk2, c1, e1, e2)
    return _residual_t(x, y_t, modr, row(layer, 5))


def kernel(x, c, positions, ada_w, ada_b, norm_g, mla_w_in, mla_g_q, mla_w_uq, mla_g_kv, mla_w_ukv, mla_g_qn, mla_g_kn, mla_w_o, dil_w_in, dil_g_qn, dil_g_kn, dil_w_o, peer_w_q, peer_sub_keys, peer_u, peer_v):
    b, s, d = x.shape
    depth = ada_w.shape[0]
    mod = _ada_mod(c, ada_w, ada_b)
    modr = mod.reshape(depth * b * 6, 1, d)
    row = lambda layer, part: (lambda bi: (layer * b + bi) * 6 + part)
    m_cos, m_sin_lo, m_sin_hi, d_cos, d_sin = _rope_tables(positions)

    for layer in range(depth):
        g1 = norm_g[layer, 0].reshape(1, d)
        g2 = norm_g[layer, 1].reshape(1, d)
        a = layer // 2
        if layer % 2 == 0:
            w_in_p, w_uq_p, g_qn_p, g_kn_p = _mla_weights(mla_w_in[a], mla_w_uq[a], mla_g_qn[a], mla_g_kn[a])
            (z,) = _norm_matmul(x, g1, modr, row(layer, 1), row(layer, 0), w_in_p, emit_h=False)
            q, k, v = _mla_prep(z, mla_g_q[a].reshape(1, -1), mla_g_kv[a].reshape(1, -1), w_uq_p,
                                mla_w_ukv[a].astype(BF16), g_qn_p, g_kn_p, (m_cos, m_sin_lo, m_sin_hi))
            o = _flash_attention(q, k, v)
            x = _proj_residual(o, mla_w_o[a].astype(BF16), x, modr, row(layer, 2))
        else:
            hmod = _norm_mod(x, g1, modr, row(layer, 1), row(layer, 0))
            w_in = dil_w_in[a].astype(BF16)
            ones = jnp.ones((len(DIL_GROUPS), DIL_HEAD_DIM), F32)
            gains = jnp.stack([dil_g_qn[a], dil_g_kn[a], ones], axis=1).reshape(-1, 1, DIL_HEAD_DIM)
            state = None
            for gi, (_, dil) in enumerate(DIL_GROUPS):
                qkv = _dil_project(hmod, w_in, gains, d_cos, d_sin, gi, dil)
                state = _dil_attention(qkv, state, is_last=gi == len(DIL_GROUPS) - 1)
            x = _proj_residual(state, dil_w_o[a].astype(BF16), x, modr, row(layer, 2))
        x = _peer(x, layer, g2, modr, row, peer_w_q[layer], peer_sub_keys[layer], peer_u[layer], peer_v[layer])
    return x
```

```python
import functools
import math

import jax
import jax.numpy as jnp
from jax import lax
from jax.experimental import pallas as pl
from jax.experimental.pallas import tpu as pltpu

F32 = jnp.float32
BF16 = jnp.bfloat16
EPS = 1e-6
ROPE_THETA = 10000.0
LOG2E = 1.4426950408889634
NEG_INF = float("-inf")
LANES = 128

MLA_HEADS = 16
MLA_Q_RANK = 512
MLA_KV_RANK = 512
MLA_NOPE = 128
MLA_ROPE = 64
MLA_QK = MLA_NOPE + MLA_ROPE
MLA_V = 128
MLA_HEAD_PAD = 256

DIL_GROUPS = ((128, 1), (512, 4), (2048, 16))
DIL_HEADS = 16
DIL_HEAD_DIM = 128
DIL_STEPS = 128

PEER_HEADS = 8
PEER_NKEYS = 128
PEER_TOPK = 16

_NT = (((1,), (1,)), ((), ()))


def _params(sem, vmem_mb=48):
    return pltpu.CompilerParams(dimension_semantics=sem, vmem_limit_bytes=vmem_mb << 20)


def _ada_kernel(c_ref, w_ref, b_ref, o_ref):
    c = c_ref[...]
    sc = c / (1.0 + jnp.exp(-c))
    o_ref[0] = jnp.dot(sc, w_ref[0], preferred_element_type=F32,
                       precision=lax.Precision.HIGHEST) + b_ref[0]


def _ada_mod(c, ada_w, ada_b, tn=768):
    depth, d, n = ada_w.shape
    b = c.shape[0]
    return pl.pallas_call(
        _ada_kernel,
        name="ada_mod",
        grid=(depth, n // tn),
        in_specs=[pl.BlockSpec((b, d), lambda l, j: (0, 0)),
                  pl.BlockSpec((1, d, tn), lambda l, j: (l, 0, j)),
                  pl.BlockSpec((1, 1, tn), lambda l, j: (l, 0, j))],
        out_specs=pl.BlockSpec((1, b, tn), lambda l, j: (l, 0, j)),
        out_shape=jax.ShapeDtypeStruct((depth, b, n), F32),
        compiler_params=_params(("parallel", "parallel")),
    )(c, ada_w, ada_b.reshape(depth, 1, n))


def _rope_tab_kernel(pos_ref, fm_ref, fd_ref, mc_ref, ms1_ref, ms2_ref, dc_ref, ds_ref):
    pos = pos_ref[0].astype(F32)
    lane = lax.broadcasted_iota(jnp.int32, (pos.shape[0], LANES), 1)
    am = pos * fm_ref[...]
    cm, sm = jnp.cos(am), jnp.sin(am)
    mc_ref[0] = jnp.where(lane < 2 * (MLA_ROPE // 2), cm, 0.0)
    ms1_ref[0] = jnp.where(lane < MLA_ROPE // 2, -sm, 0.0)
    ms2_ref[0] = jnp.where(lane < MLA_ROPE // 2, 0.0, jnp.where(lane < MLA_ROPE, sm, 0.0))
    ad = pos * fd_ref[...]
    dc_ref[0] = jnp.cos(ad)
    sd = jnp.sin(ad)
    ds_ref[0] = jnp.where(lane < DIL_HEAD_DIM // 2, -sd, sd)


def _rope_tables(positions, ts=512):
    b, s = positions.shape
    hm, hd = MLA_ROPE // 2, DIL_HEAD_DIM // 2
    inv_m = ROPE_THETA ** (-jnp.arange(hm, dtype=F32) / hm)
    inv_d = ROPE_THETA ** (-jnp.arange(hd, dtype=F32) / hd)
    fm = jnp.concatenate([inv_m, inv_m, jnp.zeros((LANES - 2 * hm,), F32)]).reshape(1, LANES)
    fd = jnp.concatenate([inv_d, inv_d]).reshape(1, LANES)
    tab = jax.ShapeDtypeStruct((b, s, LANES), F32)
    row = pl.BlockSpec((1, ts, LANES), lambda bi, i: (bi, i, 0))
    frq = pl.BlockSpec((1, LANES), lambda bi, i: (0, 0))
    return pl.pallas_call(
        _rope_tab_kernel,
        name="rope_tables",
        grid=(b, s // ts),
        in_specs=[pl.BlockSpec((1, ts, 1), lambda bi, i: (bi, i, 0)), frq, frq],
        out_specs=[row] * 5,
        out_shape=[tab] * 5,
        compiler_params=_params(("parallel", "parallel")),
    )(positions.reshape(b, s, 1), fm, fd)


def _norm_matmul_kernel(x_ref, g_ref, sc_ref, sh_ref, w_ref, o_ref, *h_out):
    x = x_ref[0]
    inv = lax.rsqrt(jnp.mean(x * x, axis=-1, keepdims=True) + EPS)
    h = (x * inv) * g_ref[...]
    h = (h * (1.0 + sc_ref[0]) + sh_ref[0]).astype(BF16)
    for h_ref in h_out:
        h_ref[0] = h
    o_ref[0] = jnp.dot(h, w_ref[...], preferred_element_type=F32)


def _norm_mod_kernel(x_ref, g_ref, sc_ref, sh_ref, o_ref):
    x = x_ref[0]
    inv = lax.rsqrt(jnp.mean(x * x, axis=-1, keepdims=True) + EPS)
    h = (x * inv) * g_ref[...]
    o_ref[0] = (h * (1.0 + sc_ref[0]) + sh_ref[0]).astype(o_ref.dtype)


def _norm_mod(x, g_row, modr, row_scale, row_shift, tm=512):
    b, s, d = x.shape
    return pl.pallas_call(
        _norm_mod_kernel,
        name="norm_mod",
        grid=(b, s // tm),
        in_specs=[pl.BlockSpec((1, tm, d), lambda bi, i: (bi, i, 0)),
                  pl.BlockSpec((1, d), lambda bi, i: (0, 0)),
                  pl.BlockSpec((1, 1, d), lambda bi, i: (row_scale(bi), 0, 0)),
                  pl.BlockSpec((1, 1, d), lambda bi, i: (row_shift(bi), 0, 0))],
        out_specs=pl.BlockSpec((1, tm, d), lambda bi, i: (bi, i, 0)),
        out_shape=jax.ShapeDtypeStruct((b, s, d), BF16),
        compiler_params=_params(("parallel", "parallel")),
    )(x, g_row, modr, modr)


def _norm_matmul(x, g_row, modr, row_scale, row_shift, w, emit_h, tm=512):
    b, s, d = x.shape
    n = w.shape[1]
    rows = lambda width: pl.BlockSpec((1, tm, width), lambda bi, i: (bi, i, 0))
    out_specs, out_shape = [rows(n)], [jax.ShapeDtypeStruct((b, s, n), F32)]
    if emit_h:
        out_specs.append(rows(d))
        out_shape.append(jax.ShapeDtypeStruct((b, s, d), BF16))
    return pl.pallas_call(
        _norm_matmul_kernel,
        name="norm_matmul",
        grid=(b, s // tm),
        in_specs=[rows(d),
                  pl.BlockSpec((1, d), lambda bi, i: (0, 0)),
                  pl.BlockSpec((1, 1, d), lambda bi, i: (row_scale(bi), 0, 0)),
                  pl.BlockSpec((1, 1, d), lambda bi, i: (row_shift(bi), 0, 0)),
                  pl.BlockSpec((d, n), lambda bi, i: (0, 0))],
        out_specs=out_specs,
        out_shape=out_shape,
        compiler_params=_params(("parallel", "parallel")),
    )(x, g_row, modr, modr, w)


def _proj_res_kernel(a_ref, w_ref, x_ref, g_ref, o_ref):
    y = jnp.dot(a_ref[0], w_ref[...], preferred_element_type=F32)
    o_ref[0] = x_ref[0] + g_ref[0] * y


def _proj_residual(a, w, x, modr, row_gate, tm=512):
    b, s, k = a.shape
    d = w.shape[1]
    tn = d
    return pl.pallas_call(
        _proj_res_kernel,
        name="proj_residual",
        grid=(b, s // tm, d // tn),
        in_specs=[pl.BlockSpec((1, tm, k), lambda bi, i, j: (bi, i, 0)),
                  pl.BlockSpec((k, tn), lambda bi, i, j: (0, j)),
                  pl.BlockSpec((1, tm, tn), lambda bi, i, j: (bi, i, j)),
                  pl.BlockSpec((1, 1, tn), lambda bi, i, j: (row_gate(bi), 0, j))],
        out_specs=pl.BlockSpec((1, tm, tn), lambda bi, i, j: (bi, i, j)),
        out_shape=jax.ShapeDtypeStruct((b, s, d), F32),
        compiler_params=_params(("parallel", "parallel", "parallel")),
    )(a, w, x, modr)


def _mla_prep_kernel(z_ref, gq_ref, gkv_ref, wuq_ref, wukv_ref, gqn_ref, gkn_ref,
                     c_ref, s1_ref, s2_ref, q_ref, k_ref, vt_ref, *, q_scale):
    z = z_ref[0]
    cq = z[:, :MLA_Q_RANK]
    ckv = z[:, MLA_Q_RANK:MLA_Q_RANK + MLA_KV_RANK]
    kr = z[:, MLA_Q_RANK + MLA_KV_RANK:]

    def rms(t, g):
        return (t * lax.rsqrt(jnp.mean(t * t, axis=-1, keepdims=True) + EPS)) * g

    qa = jnp.dot(rms(cq, gq_ref[...]).astype(BF16), wuq_ref[...], preferred_element_type=F32)
    kva = jnp.dot(rms(ckv, gkv_ref[...]).astype(BF16), wukv_ref[...], preferred_element_type=F32)
    cos, sin_lo, sin_hi = c_ref[0], s1_ref[0], s2_ref[0]

    def rope(t):
        return (t * cos + pltpu.roll(t, LANES - MLA_ROPE // 2, 1) * sin_lo
                + pltpu.roll(t, MLA_ROPE // 2, 1) * sin_hi)

    gqn, gkn = gqn_ref[...], gkn_ref[...]
    kr_ss = jnp.sum(kr * kr, axis=-1, keepdims=True)
    kr_rot = rope(kr * gkn[:, MLA_NOPE:])
    for h in range(MLA_HEADS):
        lo = h * MLA_HEAD_PAD
        qh = qa[:, lo:lo + MLA_HEAD_PAD]
        inv = lax.rsqrt(jnp.sum(qh * qh, axis=-1, keepdims=True) * (1.0 / MLA_QK) + EPS) * q_scale
        q_ref[0, h, :, :MLA_NOPE] = ((qh[:, :MLA_NOPE] * gqn[:, :MLA_NOPE]) * inv).astype(BF16)
        q_ref[0, h, :, MLA_NOPE:] = (rope(qh[:, MLA_NOPE:] * gqn[:, MLA_NOPE:]) * inv).astype(BF16)
        kn = kva[:, lo:lo + MLA_NOPE]
        kss = jnp.sum(kn * kn, axis=-1, keepdims=True) + kr_ss
        kinv = lax.rsqrt(kss * (1.0 / MLA_QK) + EPS)
        k_ref[0, h, :, :MLA_NOPE] = ((kn * gkn[:, :MLA_NOPE]) * kinv).astype(BF16)
        k_ref[0, h, :, MLA_NOPE:] = (kr_rot * kinv).astype(BF16)
        vt_ref[0, h] = kva[:, lo + MLA_NOPE:lo + MLA_HEAD_PAD].T.astype(BF16)


def _mla_prep(z, g_q, g_kv, w_uq_p, w_ukv, g_qn_p, g_kn_p, tabs, tm=256):
    b, s, zw = z.shape
    nh = MLA_HEADS
    row = lambda w: pl.BlockSpec((1, w), lambda bi, i: (0, 0))
    full = lambda a: pl.BlockSpec(a.shape, lambda bi, i: (0, 0))
    tab = pl.BlockSpec((1, tm, LANES), lambda bi, i: (bi, i, 0))
    head = lambda w: pl.BlockSpec((1, nh, tm, w), lambda bi, i: (bi, 0, i, 0))
    kern = functools.partial(_mla_prep_kernel, q_scale=MLA_QK ** -0.5 * LOG2E)
    return pl.pallas_call(
        kern,
        name="mla_prep",
        grid=(b, s // tm),
        in_specs=[pl.BlockSpec((1, tm, zw), lambda bi, i: (bi, i, 0)),
                  row(MLA_Q_RANK), row(MLA_KV_RANK), full(w_uq_p), full(w_ukv),
                  row(MLA_HEAD_PAD), row(MLA_HEAD_PAD), tab, tab, tab],
        out_specs=[head(MLA_HEAD_PAD), head(MLA_HEAD_PAD),
                   pl.BlockSpec((1, nh, MLA_V, tm), lambda bi, i: (bi, 0, 0, i))],
        out_shape=[jax.ShapeDtypeStruct((b, nh, s, MLA_HEAD_PAD), BF16),
                   jax.ShapeDtypeStruct((b, nh, s, MLA_HEAD_PAD), BF16),
                   jax.ShapeDtypeStruct((b, nh, MLA_V, s), BF16)],
        compiler_params=_params(("parallel", "parallel")),
    )(z, g_q, g_kv, w_uq_p, w_ukv, g_qn_p, g_kn_p, *tabs)


def _flash_kernel(q_ref, k_ref, vt_ref, o_ref, acc_ref, *, t, nsub):
    qi = pl.program_id(2)
    acc_ref[...] = jnp.zeros(acc_ref.shape, F32)
    qs = [q_ref[0, 0, c * t:(c + 1) * t, :] for c in range(nsub)]

    def scores(c, j, diagonal):
        start = pl.multiple_of(j * t, t)
        k = k_ref[0, 0, pl.ds(start, t), :]
        st = lax.dot_general(k, qs[c], _NT, preferred_element_type=F32)
        if diagonal:
            key = lax.broadcasted_iota(jnp.int32, (t, t), 0)
            qry = lax.broadcasted_iota(jnp.int32, (t, t), 1)
            st = jnp.where(key <= qry, st, NEG_INF)
        return st

    def soft(st, m, l):
        m_new = jnp.maximum(m, jnp.max(st, axis=0, keepdims=True))
        p = jnp.exp2(st - m_new)
        alpha = jnp.exp2(m - m_new)
        l_new = alpha * l + jnp.sum(p, axis=0, keepdims=True)
        return p.astype(BF16), alpha, m_new, l_new

    def accum(c, j, p, alpha):
        start = pl.multiple_of(j * t, t)
        vt = vt_ref[0, 0, :, pl.ds(start, t)]
        acc_ref[c] = alpha * acc_ref[c] + jnp.dot(vt, p, preferred_element_type=F32)

    def multi(chains, j, carry, diag_chain):
        carry = list(carry)
        sts = [scores(c, j, c == diag_chain) for c in chains]
        ps = []
        for c, st in zip(chains, sts):
            p, alpha, carry[2 * c], carry[2 * c + 1] = soft(st, carry[2 * c], carry[2 * c + 1])
            ps.append((p, alpha))
        for c, (p, alpha) in zip(chains, ps):
            accum(c, j, p, alpha)
        return tuple(carry)

    def body(j, carry):
        return multi(range(nsub), j, carry, -1)

    init = []
    for c in range(nsub):
        init += [jnp.full((1, t), NEG_INF, F32), jnp.zeros((1, t), F32)]
    carry = lax.fori_loop(0, nsub * qi, body, tuple(init))
    for d in range(nsub):
        carry = multi(range(d, nsub), nsub * qi + d, carry, d)
    for c in range(nsub):
        ot = acc_ref[c] / carry[2 * c + 1]
        o_ref[0, c * t:(c + 1) * t, :] = ot.T.astype(o_ref.dtype)


def _flash_attention(q, k, vt, t=512, nsub=8):
    b, nh, s, dq = q.shape
    dv = vt.shape[2]
    tq = t * nsub
    return pl.pallas_call(
        functools.partial(_flash_kernel, t=t, nsub=nsub),
        name="mla_flash",
        grid=(b, nh, s // tq),
        in_specs=[pl.BlockSpec((1, 1, tq, dq), lambda bi, h, i: (bi, h, i, 0)),
                  pl.BlockSpec((1, 1, s, dq), lambda bi, h, i: (bi, h, 0, 0)),
                  pl.BlockSpec((1, 1, dv, s), lambda bi, h, i: (bi, h, 0, 0))],
        out_specs=pl.BlockSpec((1, tq, dv), lambda bi, h, i: (bi, i, h)),
        out_shape=jax.ShapeDtypeStruct((b, s, nh * dv), BF16),
        scratch_shapes=[pltpu.VMEM((nsub, dv, t), F32)],
        compiler_params=_params(("parallel", "parallel", "arbitrary")),
    )(q, k, vt)


_DIL_TN = 1024
_DIL_SUB = 256
_DIL_HPT = _DIL_TN // DIL_HEAD_DIM
_DIL_TPP = DIL_HEADS // _DIL_HPT


def _dil_proj_kernel(h_ref, w_ref, gain_ref, c_ref, s_ref, o_ref, *, q_scale):
    part = pl.program_id(3) // _DIL_TPP
    nsub = _DIL_TN // _DIL_SUB
    hps = _DIL_SUB // LANES
    h = h_ref[0]
    ys = [jnp.dot(h, w_ref[:, sb * _DIL_SUB:(sb + 1) * _DIL_SUB], preferred_element_type=F32)
          for sb in range(nsub)]
    is_v = part == 2
    cos, sin = c_ref[0], s_ref[0]
    gain = gain_ref[0]
    post = jnp.where(part == 0, q_scale, 1.0)
    ri = lax.broadcasted_iota(jnp.int32, (_DIL_SUB, _DIL_SUB), 0) // LANES
    ci = lax.broadcasted_iota(jnp.int32, (_DIL_SUB, _DIL_SUB), 1) // LANES
    seg = jnp.where(ri == ci, 1.0, 0.0).astype(BF16)
    for sb in range(nsub):
        y = ys[sb]
        ss = jnp.dot((y * y).astype(BF16), seg, preferred_element_type=F32)
        inv = lax.rsqrt(ss * (1.0 / DIL_HEAD_DIM) + EPS)
        for hh in range(hps):
            cols = slice(hh * LANES, (hh + 1) * LANES)
            yh = y[:, cols]
            yn = (yh * inv[:, cols]) * gain
            rot = yn * cos + pltpu.roll(yn, DIL_HEAD_DIM // 2, 1) * sin
            o_ref[0, 0, 0, sb * hps + hh] = jnp.where(is_v, yh, rot * post).astype(BF16)


def _dil_project(h, w_in, gains, cos_t, sin_t, group, dil):
    b, s, d = h.shape
    l = s // dil
    tl = min(512, l)
    tiles = 3 * _DIL_TPP
    kern = functools.partial(_dil_proj_kernel, q_scale=DIL_HEAD_DIM ** -0.5 * LOG2E)
    return pl.pallas_call(
        kern,
        name=f"dil_proj_d{dil}",
        grid=(b, dil, l // tl, tiles),
        in_specs=[pl.BlockSpec((1, tl, d), lambda bi, r, i, j: (bi, i, r)),
                  pl.BlockSpec((d, _DIL_TN), lambda bi, r, i, j: (0, group * tiles + j)),
                  pl.BlockSpec((1, 1, LANES), lambda bi, r, i, j: (group * 3 + j // _DIL_TPP, 0, 0)),
                  pl.BlockSpec((1, tl, LANES), lambda bi, r, i, j: (bi, i, r)),
                  pl.BlockSpec((1, tl, LANES), lambda bi, r, i, j: (bi, i, r))],
        out_specs=pl.BlockSpec((1, 1, 1, _DIL_HPT, tl, LANES),
                               lambda bi, r, i, j: (bi, r, j // _DIL_TPP, j % _DIL_TPP, i, 0)),
        out_shape=jax.ShapeDtypeStruct((b, dil, 3, DIL_HEADS, l, LANES), BF16),
        compiler_params=_params(("parallel", "parallel", "parallel", "arbitrary")),
    )(h.reshape(b, l, dil * d), w_in, gains,
      cos_t.reshape(b, l, dil * LANES), sin_t.reshape(b, l, dil * LANES))


def _dil_attn_kernel(*refs, tq, has_state, is_last):
    q_ref, kc_ref, kp_ref, vc_ref, vp_ref = refs[:5]
    refs = refs[5:]
    if has_state:
        acc_in, m_in, l_in = refs[:3]
        refs = refs[3:]
    if is_last:
        (o_ref,) = refs
    else:
        acc_out, m_out, l_out = refs

    first = pl.program_id(2) == 0
    row = lax.broadcasted_iota(jnp.int32, (tq, tq), 0)
    col = lax.broadcasted_iota(jnp.int32, (tq, tq), 1)
    dist = row - col
    bias_c = jnp.where(dist >= 0, jnp.where(dist <= DIL_STEPS, 0.0, NEG_INF), NEG_INF)
    rowp = lax.broadcasted_iota(jnp.int32, (tq, DIL_STEPS), 0)
    colp = lax.broadcasted_iota(jnp.int32, (tq, DIL_STEPS), 1)
    bias_p = jnp.where(colp >= rowp, jnp.where(first, NEG_INF, 0.0), NEG_INF)
    lane = lax.broadcasted_iota(jnp.int32, (tq, LANES), 1)
    if has_state:
        m_old, l_old = m_in[0], l_in[0]
    m_tile = jnp.zeros((tq, LANES), F32)
    l_tile = jnp.zeros((tq, LANES), F32)

    for h in range(DIL_HEADS):
        q = q_ref[0, 0, 0, h]
        sc = lax.dot_general(q, kc_ref[0, 0, 0, h], _NT, preferred_element_type=F32) + bias_c
        sp = lax.dot_general(q, kp_ref[0, 0, 0, h], _NT, preferred_element_type=F32) + bias_p
        m_new = jnp.maximum(jnp.max(sc, axis=-1, keepdims=True), jnp.max(sp, axis=-1, keepdims=True))
        if has_state:
            m_prev = m_old[:, h:h + 1]
            m_new = jnp.maximum(m_new, m_prev)
            alpha = jnp.exp2(m_prev - m_new)
        pc = jnp.exp2(sc - m_new)
        pp = jnp.exp2(sp - m_new)
        l_new = jnp.sum(pc, axis=-1, keepdims=True) + jnp.sum(pp, axis=-1, keepdims=True)
        acc = (jnp.dot(pc.astype(BF16), vc_ref[0, 0, 0, h], preferred_element_type=F32)
               + jnp.dot(pp.astype(BF16), vp_ref[0, 0, 0, h], preferred_element_type=F32))
        cols = slice(h * LANES, (h + 1) * LANES)
        if has_state:
            l_new = l_new + alpha * l_old[:, h:h + 1]
            acc = acc + alpha * acc_in[0, :, cols]
        if is_last:
            o_ref[0, :, cols] = (acc / l_new).astype(o_ref.dtype)
        else:
            acc_out[0, :, cols] = acc
            m_tile = jnp.where(lane == h, m_new, m_tile)
            l_tile = jnp.where(lane == h, l_new, l_tile)
    if not is_last:
        m_out[0] = m_tile
        l_out[0] = l_tile


def _dil_attention(qkv, state, is_last, tq=256):
    b, dil, _, nh, l, dh = qkv.shape
    tq = min(tq, l)
    width = nh * dh
    ratio = tq // DIL_STEPS
    cur = lambda part: pl.BlockSpec((1, 1, 1, nh, tq, dh), lambda bi, r, i: (bi, r, part, 0, i, 0))
    prev = lambda part: pl.BlockSpec(
        (1, 1, 1, nh, DIL_STEPS, dh),
        lambda bi, r, i: (bi, r, part, 0, jnp.maximum(i * ratio - 1, 0), 0))
    wide = pl.BlockSpec((1, tq, width), lambda bi, r, i: (bi, i, r))
    thin = pl.BlockSpec((1, tq, LANES), lambda bi, r, i: (bi, i, r))
    in_specs = [cur(0), cur(1), prev(1), cur(2), prev(2)]
    args = [qkv] * 5
    if state is not None:
        in_specs += [wide, thin, thin]
        args += [state[0].reshape(b, l, dil * width), state[1].reshape(b, l, dil * LANES),
                 state[2].reshape(b, l, dil * LANES)]
    if is_last:
        out_specs = wide
        out_shape = jax.ShapeDtypeStruct((b, l, dil * width), BF16)
    else:
        out_specs = [wide, thin, thin]
        out_shape = [jax.ShapeDtypeStruct((b, l, dil * width), F32),
                     jax.ShapeDtypeStruct((b, l, dil * LANES), F32),
                     jax.ShapeDtypeStruct((b, l, dil * LANES), F32)]
    kern = functools.partial(_dil_attn_kernel, tq=tq, has_state=state is not None, is_last=is_last)
    out = pl.pallas_call(
        kern,
        name=f"dil_attn_d{dil}",
        grid=(b, dil, l // tq),
        in_specs=in_specs,
        out_specs=out_specs,
        out_shape=out_shape,
        compiler_params=_params(("parallel", "parallel", "parallel")),
    )(*args)
    s = l * dil
    if is_last:
        return out.reshape(b, s, width)
    return (out[0].reshape(b, s, width), out[1].reshape(b, s, LANES), out[2].reshape(b, s, LANES))


def _batcher_pairs(n):
    pairs = []
    p = 1
    while p < n:
        k = p
        while k >= 1:
            for j in range(k % p, n - k, 2 * k):
                for i in range(min(k, n - j - k)):
                    if (i + j) // (2 * p) == (i + j + k) // (2 * p):
                        pairs.append((i + j, i + j + k))
            k //= 2
        p *= 2
    return pairs


def _sorted_top(vs, count):
    vs = list(vs) + [None] * (count - len(vs))

    def exchange(i, j):
        a, b = vs[i], vs[j]
        if b is None:
            return
        if a is None:
            vs[i], vs[j] = b, None
            return
        vs[i], vs[j] = jnp.maximum(a, b), jnp.minimum(a, b)

    for i, j in _batcher_pairs(count):
        exchange(i, j)
    shape = next(v.shape for v in vs if v is not None)
    vs = [jnp.full(shape, NEG_INF, F32) if v is None else v for v in vs]
    for shift in (4, 2, 1):
        other = [pltpu.roll(v, shift, 0) for v in vs]
        vs = [jnp.maximum(vs[i], other[count - 1 - i]) for i in range(count)]
        k = count // 2
        while k >= 1:
            for i in range(count):
                if i & k == 0:
                    exchange(i, i + k)
            k //= 2
    return vs


def _peer_topk_kernel(q_ref, keys_ref, rk2_ref, c1_ref, e1_ref, e2_ref):
    k = PEER_TOPK
    tt = q_ref.shape[0]
    nslab = PEER_NKEYS // 8
    sub = lax.broadcasted_iota(jnp.int32, (8, tt), 0)

    def by_sublane(vals):
        out = vals[0]
        for i in range(1, 8):
            out = jnp.where(sub == i, vals[i], out)
        return out

    def count(slab, thr):
        return jnp.sum(jnp.where(slab >= thr, 1.0, 0.0), axis=0, keepdims=True)

    for h in range(PEER_HEADS):
        sts, tops = [], []
        for p in range(2):
            hp = 2 * h + p
            qs = q_ref[:, hp * PEER_NKEYS:(hp + 1) * PEER_NKEYS]
            st = lax.dot_general(keys_ref[hp], qs, _NT, preferred_element_type=F32,
                                 precision=lax.Precision.HIGHEST)
            slabs = [st[8 * i:8 * i + 8, :] for i in range(nslab)]
            sts.append(slabs)
            tops.append(_sorted_top(slabs, k))
        a, bv = tops
        b_lo, b_hi, a_hi = by_sublane(bv[:8]), by_sublane(bv[8:]), by_sublane(a[8:])
        cand = [a[0] + b_lo, a[0] + b_hi, a[1] + b_lo]
        for i in range(2, 8):
            cand.append(jnp.where(sub < k // (i + 1), a[i] + b_lo, NEG_INF))
        cand.append(a_hi + bv[0])
        best = _sorted_top(cand, k)
        top, thr = best[0], best[k - 1]
        z = jnp.exp(best[0] - top)
        for r in range(1, k):
            z = z + jnp.exp(best[r] - top)
        cnt = [count(cand[0], thr) + count(cand[1], thr)]
        cnt += [count(cand[i + 1], thr) for i in range(1, 8)]
        tail = jnp.where(cand[9] >= thr, 1.0, 0.0)
        cnt += [tail[i:i + 1] for i in range(8)]
        inv_z = 1.0 / z
        for i in range(nslab):
            s1, s2 = sts[0][i], sts[1][i]
            c1 = jnp.zeros(s1.shape, F32)
            rk = jnp.full(s2.shape, float(k), F32)
            for r in range(k - 1, -1, -1):
                c1 = jnp.where(s1 >= a[r], cnt[r], c1)
                rk = jnp.where(s2 >= bv[r], float(r), rk)
            rows = slice(8 * i, 8 * i + 8)
            c1_ref[h, rows, :] = c1
            rk2_ref[h, rows, :] = rk.astype(BF16)
            e1_ref[h, rows, :] = jnp.exp(s1 - a[0]) * inv_z
            e2_ref[h, rows, :] = jnp.exp(s2 - bv[0]).astype(BF16)


def _peer_topk(q2d, sub_keys, tt=512):
    t = q2d.shape[0]
    nh = PEER_HEADS
    keys = sub_keys.reshape(2 * nh, PEER_NKEYS, sub_keys.shape[-1])
    big = pl.BlockSpec((nh, PEER_NKEYS, tt), lambda i: (0, 0, i))
    return pl.pallas_call(
        _peer_topk_kernel,
        name="peer_topk",
        grid=(t // tt,),
        in_specs=[pl.BlockSpec((tt, q2d.shape[1]), lambda i: (i, 0)),
                  pl.BlockSpec(keys.shape, lambda i: (0, 0, 0))],
        out_specs=[big, big, big, big],
        out_shape=[jax.ShapeDtypeStruct((nh, PEER_NKEYS, t), BF16),
                   jax.ShapeDtypeStruct((nh, PEER_NKEYS, t), F32),
                   jax.ShapeDtypeStruct((nh, PEER_NKEYS, t), F32),
                   jax.ShapeDtypeStruct((nh, PEER_NKEYS, t), BF16)],
        compiler_params=_params(("parallel",)),
    )(q2d, keys)


def _peer_dense_kernel(x_ref, u_ref, vt_ref, rk2_ref, c1_ref, e1_ref, e2_ref, o_ref, act_ref, *, te):
    j = pl.program_id(1)

    @pl.when(j == 0)
    def _():
        o_ref[...] = jnp.zeros(o_ref.shape, F32)

    half = te // 2
    per = half // PEER_NKEYS
    pre = [lax.dot_general(u_ref[sb * half:(sb + 1) * half, :], x_ref[...], _NT,
                           preferred_element_type=F32) for sb in range(2)]
    for sb in range(2):
        a = pre[sb]
        ge = (0.5 * a * (1.0 + lax.erf(a * (1.0 / math.sqrt(2.0))))).astype(BF16)
        for ii in range(per):
            i1 = j * (2 * per) + sb * per + ii
            gate = jnp.zeros((PEER_NKEYS, x_ref.shape[0]), BF16)
            for h in range(PEER_HEADS):
                c1 = c1_ref[h, pl.ds(i1, 1), :].astype(BF16)
                e1 = e1_ref[h, pl.ds(i1, 1), :].astype(BF16)
                gate = gate + jnp.where(rk2_ref[h] < c1, e2_ref[h], jnp.zeros((), BF16)) * e1
            lo = sb * half + ii * PEER_NKEYS
            act_ref[lo:lo + PEER_NKEYS, :] = ge[ii * PEER_NKEYS:(ii + 1) * PEER_NKEYS, :] * gate
    o_ref[...] += jnp.dot(vt_ref[...], act_ref[...], preferred_element_type=F32)


def _peer_dense(h2d, u_bf, vt_bf, rk2, c1, e1, e2, tt=512, te=1024):
    t, d = h2d.shape
    ne = u_bf.shape[0]
    big = pl.BlockSpec((PEER_HEADS, PEER_NKEYS, tt), lambda i, j: (0, 0, i))
    return pl.pallas_call(
        functools.partial(_peer_dense_kernel, te=te),
        name="peer_dense",
        grid=(t // tt, ne // te),
        in_specs=[pl.BlockSpec((tt, d), lambda i, j: (i, 0)),
                  pl.BlockSpec((te, d), lambda i, j: (j, 0)),
                  pl.BlockSpec((d, te), lambda i, j: (0, j)),
                  big, big, big, big],
        out_specs=pl.BlockSpec((d, tt), lambda i, j: (0, i)),
        out_shape=jax.ShapeDtypeStruct((d, t), F32),
        scratch_shapes=[pltpu.VMEM((te, tt), BF16)],
        compiler_params=_params(("parallel", "arbitrary"), vmem_mb=56),
    )(h2d, u_bf, vt_bf, rk2, c1, e1, e2)


def _res_t_kernel(x_ref, yt_ref, g_ref, o_ref):
    o_ref[0] = x_ref[0] + g_ref[0] * yt_ref[...].T


def _residual_t(x, y_t, modr, row_gate, tm=512, tn=512):
    b, s, d = x.shape
    nt = s // tm
    return pl.pallas_call(
        _res_t_kernel,
        name="residual_t",
        grid=(b, nt, d // tn),
        in_specs=[pl.BlockSpec((1, tm, tn), lambda bi, i, j: (bi, i, j)),
                  pl.BlockSpec((tn, tm), lambda bi, i, j: (j, bi * nt + i)),
                  pl.BlockSpec((1, 1, tn), lambda bi, i, j: (row_gate(bi), 0, j))],
        out_specs=pl.BlockSpec((1, tm, tn), lambda bi, i, j: (bi, i, j)),
        out_shape=jax.ShapeDtypeStruct((b, s, d), F32),
        compiler_params=_params(("parallel", "parallel", "parallel")),
    )(x, y_t, modr)


def _pad_cols(a, width):
    return jnp.pad(a, ((0, 0), (0, width - a.shape[1])))


def _mla_weights(w_in, w_uq, g_qn, g_kn):
    w_in_p = _pad_cols(w_in, MLA_Q_RANK + MLA_KV_RANK + LANES).astype(BF16)
    w_uq_p = jnp.pad(w_uq.reshape(MLA_Q_RANK, MLA_HEADS, MLA_QK),
                     ((0, 0), (0, 0), (0, MLA_HEAD_PAD - MLA_QK)))
    w_uq_p = w_uq_p.reshape(MLA_Q_RANK, MLA_HEADS * MLA_HEAD_PAD).astype(BF16)
    pad = lambda g: _pad_cols(g.reshape(1, MLA_QK), MLA_HEAD_PAD)
    return w_in_p, w_uq_p, pad(g_qn), pad(g_kn)


def _peer(x, layer, norm_row, modr, row, w_q, sub_keys, u_tab, v_tab):
    b, s, d = x.shape
    q, hmod = _norm_matmul(x, norm_row, modr, row(layer, 4), row(layer, 3), w_q.astype(BF16), emit_h=True)
    rk2, c1, e1, e2 = _peer_topk(q.reshape(b * s, -1), sub_keys)
    y_t = _peer_dense(hmod.reshape(b * s, d), u_tab.astype(BF16), v_tab.T.astype(BF16), rk2, c1, e1, e2)
    return _residual_t(x, y_t, modr, row(layer, 5))


def kernel(x, c, positions, ada_w, ada_b, norm_g, mla_w_in, mla_g_q, mla_w_uq, mla_g_kv, mla_w_ukv, mla_g_qn, mla_g_kn, mla_w_o, dil_w_in, dil_g_qn, dil_g_kn, dil_w_o, peer_w_q, peer_sub_keys, peer_u, peer_v):
    b, s, d = x.shape
    depth = ada_w.shape[0]
    mod = _ada_mod(c, ada_w, ada_b)
    modr = mod.reshape(depth * b * 6, 1, d)
    row = lambda layer, part: (lambda bi: (layer * b + bi) * 6 + part)
    m_cos, m_sin_lo, m_sin_hi, d_cos, d_sin = _rope_tables(positions)

    for layer in range(depth):
        g1 = norm_g[layer, 0].reshape(1, d)
        g2 = norm_g[layer, 1].reshape(1, d)
        a = layer // 2
        if layer % 2 == 0:
            w_in_p, w_uq_p, g_qn_p, g_kn_p = _mla_weights(mla_w_in[a], mla_w_uq[a], mla_g_qn[a], mla_g_kn[a])
            (z,) = _norm_matmul(x, g1, modr, row(layer, 1), row(layer, 0), w_in_p, emit_h=False)
            q, k, v = _mla_prep(z, mla_g_q[a].reshape(1, -1), mla_g_kv[a].reshape(1, -1), w_uq_p,
                                mla_w_ukv[a].astype(BF16), g_qn_p, g_kn_p, (m_cos, m_sin_lo, m_sin_hi))
            o = _flash_attention(q, k, v)
            x = _proj_residual(o, mla_w_o[a].astype(BF16), x, modr, row(layer, 2))
        else:
            hmod = _norm_mod(x, g1, modr, row(layer, 1), row(layer, 0))
            w_in = dil_w_in[a].astype(BF16)
            ones = jnp.ones((len(DIL_GROUPS), DIL_HEAD_DIM), F32)
            gains = jnp.stack([dil_g_qn[a], dil_g_kn[a], ones], axis=1).reshape(-1, 1, DIL_HEAD_DIM)
            state = None
            for gi, (_, dil) in enumerate(DIL_GROUPS):
                qkv = _dil_project(hmod, w_in, gains, d_cos, d_sin, gi, dil)
                state = _dil_attention(qkv, state, is_last=gi == len(DIL_GROUPS) - 1)
            x = _proj_residual(state, dil_w_o[a].astype(BF16), x, modr, row(layer, 2))
        x = _peer(x, layer, g2, modr, row, peer_w_q[layer], peer_sub_keys[layer], peer_u[layer], peer_v[layer])
    return x
```

```python
import functools
import math

import jax
import jax.numpy as jnp
from jax import lax
from jax.experimental import pallas as pl
from jax.experimental.pallas import tpu as pltpu

F32 = jnp.float32
BF16 = jnp.bfloat16
EPS = 1e-6
ROPE_THETA = 10000.0
LOG2E = 1.4426950408889634
NEG_INF = float("-inf")
LANES = 128

MLA_HEADS = 16
MLA_Q_RANK = 512
MLA_KV_RANK = 512
MLA_NOPE = 128
MLA_ROPE = 64
MLA_QK = MLA_NOPE + MLA_ROPE
MLA_V = 128
MLA_HEAD_PAD = 256

DIL_GROUPS = ((128, 1), (512, 4), (2048, 16))
DIL_HEADS = 16
DIL_HEAD_DIM = 128
DIL_STEPS = 128

PEER_HEADS = 8
PEER_NKEYS = 128
PEER_TOPK = 16

_NT = (((1,), (1,)), ((), ()))


def _params(sem, vmem_mb=48):
    return pltpu.CompilerParams(dimension_semantics=sem, vmem_limit_bytes=vmem_mb << 20)


def _ada_kernel(c_ref, w_ref, b_ref, o_ref):
    c = c_ref[...]
    sc = c / (1.0 + jnp.exp(-c))
    o_ref[0] = jnp.dot(sc, w_ref[0], preferred_element_type=F32,
                       precision=lax.Precision.HIGHEST) + b_ref[0]


def _ada_mod(c, ada_w, ada_b, tn=768):
    depth, d, n = ada_w.shape
    b = c.shape[0]
    return pl.pallas_call(
        _ada_kernel,
        name="ada_mod",
        grid=(depth, n // tn),
        in_specs=[pl.BlockSpec((b, d), lambda l, j: (0, 0)),
                  pl.BlockSpec((1, d, tn), lambda l, j: (l, 0, j)),
                  pl.BlockSpec((1, 1, tn), lambda l, j: (l, 0, j))],
        out_specs=pl.BlockSpec((1, b, tn), lambda l, j: (l, 0, j)),
        out_shape=jax.ShapeDtypeStruct((depth, b, n), F32),
        compiler_params=_params(("parallel", "parallel")),
    )(c, ada_w, ada_b.reshape(depth, 1, n))


def _rope_tab_kernel(pos_ref, fm_ref, fd_ref, mc_ref, ms1_ref, ms2_ref, dc_ref, ds_ref):
    pos = pos_ref[0].astype(F32)
    lane = lax.broadcasted_iota(jnp.int32, (pos.shape[0], LANES), 1)
    am = pos * fm_ref[...]
    cm, sm = jnp.cos(am), jnp.sin(am)
    mc_ref[0] = jnp.where(lane < 2 * (MLA_ROPE // 2), cm, 0.0)
    ms1_ref[0] = jnp.where(lane < MLA_ROPE // 2, -sm, 0.0)
    ms2_ref[0] = jnp.where(lane < MLA_ROPE // 2, 0.0, jnp.where(lane < MLA_ROPE, sm, 0.0))
    ad = pos * fd_ref[...]
    dc_ref[0] = jnp.cos(ad)
    sd = jnp.sin(ad)
    ds_ref[0] = jnp.where(lane < DIL_HEAD_DIM // 2, -sd, sd)


def _rope_tables(positions, ts=512):
    b, s = positions.shape
    hm, hd = MLA_ROPE // 2, DIL_HEAD_DIM // 2
    inv_m = ROPE_THETA ** (-jnp.arange(hm, dtype=F32) / hm)
    inv_d = ROPE_THETA ** (-jnp.arange(hd, dtype=F32) / hd)
    fm = jnp.concatenate([inv_m, inv_m, jnp.zeros((LANES - 2 * hm,), F32)]).reshape(1, LANES)
    fd = jnp.concatenate([inv_d, inv_d]).reshape(1, LANES)
    tab = jax.ShapeDtypeStruct((b, s, LANES), F32)
    row = pl.BlockSpec((1, ts, LANES), lambda bi, i: (bi, i, 0))
    frq = pl.BlockSpec((1, LANES), lambda bi, i: (0, 0))
    return pl.pallas_call(
        _rope_tab_kernel,
        name="rope_tables",
        grid=(b, s // ts),
        in_specs=[pl.BlockSpec((1, ts, 1), lambda bi, i: (bi, i, 0)), frq, frq],
        out_specs=[row] * 5,
        out_shape=[tab] * 5,
        compiler_params=_params(("parallel", "parallel")),
    )(positions.reshape(b, s, 1), fm, fd)


def _norm_matmul_kernel(x_ref, g_ref, sc_ref, sh_ref, w_ref, o_ref, *h_out):
    x = x_ref[0]
    inv = lax.rsqrt(jnp.mean(x * x, axis=-1, keepdims=True) + EPS)
    h = (x * inv) * g_ref[...]
    h = (h * (1.0 + sc_ref[0]) + sh_ref[0]).astype(BF16)
    for h_ref in h_out:
        h_ref[0] = h
    o_ref[0] = jnp.dot(h, w_ref[...], preferred_element_type=F32)


def _norm_mod_kernel(x_ref, g_ref, sc_ref, sh_ref, o_ref):
    x = x_ref[0]
    inv = lax.rsqrt(jnp.mean(x * x, axis=-1, keepdims=True) + EPS)
    h = (x * inv) * g_ref[...]
    o_ref[0] = (h * (1.0 + sc_ref[0]) + sh_ref[0]).astype(o_ref.dtype)


def _norm_mod(x, g_row, modr, row_scale, row_shift, tm=512):
    b, s, d = x.shape
    return pl.pallas_call(
        _norm_mod_kernel,
        name="norm_mod",
        grid=(b, s // tm),
        in_specs=[pl.BlockSpec((1, tm, d), lambda bi, i: (bi, i, 0)),
                  pl.BlockSpec((1, d), lambda bi, i: (0, 0)),
                  pl.BlockSpec((1, 1, d), lambda bi, i: (row_scale(bi), 0, 0)),
                  pl.BlockSpec((1, 1, d), lambda bi, i: (row_shift(bi), 0, 0))],
        out_specs=pl.BlockSpec((1, tm, d), lambda bi, i: (bi, i, 0)),
        out_shape=jax.ShapeDtypeStruct((b, s, d), BF16),
        compiler_params=_params(("parallel", "parallel")),
    )(x, g_row, modr, modr)


def _norm_matmul(x, g_row, modr, row_scale, row_shift, w, emit_h, tm=512):
    b, s, d = x.shape
    n = w.shape[1]
    rows = lambda width: pl.BlockSpec((1, tm, width), lambda bi, i: (bi, i, 0))
    out_specs, out_shape = [rows(n)], [jax.ShapeDtypeStruct((b, s, n), F32)]
    if emit_h:
        out_specs.append(rows(d))
        out_shape.append(jax.ShapeDtypeStruct((b, s, d), BF16))
    return pl.pallas_call(
        _norm_matmul_kernel,
        name="norm_matmul",
        grid=(b, s // tm),
        in_specs=[rows(d),
                  pl.BlockSpec((1, d), lambda bi, i: (0, 0)),
                  pl.BlockSpec((1, 1, d), lambda bi, i: (row_scale(bi), 0, 0)),
                  pl.BlockSpec((1, 1, d), lambda bi, i: (row_shift(bi), 0, 0)),
                  pl.BlockSpec((d, n), lambda bi, i: (0, 0))],
        out_specs=out_specs,
        out_shape=out_shape,
        compiler_params=_params(("parallel", "parallel")),
    )(x, g_row, modr, modr, w)


def _proj_res_kernel(a_ref, w_ref, x_ref, g_ref, o_ref):
    y = jnp.dot(a_ref[0], w_ref[...], preferred_element_type=F32)
    o_ref[0] = x_ref[0] + g_ref[0] * y


def _proj_residual(a, w, x, modr, row_gate, tm=512):
    b, s, k = a.shape
    d = w.shape[1]
    tn = d
    return pl.pallas_call(
        _proj_res_kernel,
        name="proj_residual",
        grid=(b, s // tm, d // tn),
        in_specs=[pl.BlockSpec((1, tm, k), lambda bi, i, j: (bi, i, 0)),
                  pl.BlockSpec((k, tn), lambda bi, i, j: (0, j)),
                  pl.BlockSpec((1, tm, tn), lambda bi, i, j: (bi, i, j)),
                  pl.BlockSpec((1, 1, tn), lambda bi, i, j: (row_gate(bi), 0, j))],
        out_specs=pl.BlockSpec((1, tm, tn), lambda bi, i, j: (bi, i, j)),
        out_shape=jax.ShapeDtypeStruct((b, s, d), F32),
        compiler_params=_params(("parallel", "parallel", "parallel")),
    )(a, w, x, modr)


def _mla_prep_kernel(z_ref, gq_ref, gkv_ref, wuq_ref, wukv_ref, gqn_ref, gkn_ref,
                     c_ref, s1_ref, s2_ref, q_ref, k_ref, vt_ref, *, q_scale):
    z = z_ref[0]
    cq = z[:, :MLA_Q_RANK]
    ckv = z[:, MLA_Q_RANK:MLA_Q_RANK + MLA_KV_RANK]
    kr = z[:, MLA_Q_RANK + MLA_KV_RANK:]

    def rms(t, g):
        return (t * lax.rsqrt(jnp.mean(t * t, axis=-1, keepdims=True) + EPS)) * g

    qa = jnp.dot(rms(cq, gq_ref[...]).astype(BF16), wuq_ref[...], preferred_element_type=F32)
    kva = jnp.dot(rms(ckv, gkv_ref[...]).astype(BF16), wukv_ref[...], preferred_element_type=F32)
    cos, sin_lo, sin_hi = c_ref[0], s1_ref[0], s2_ref[0]

    def rope(t):
        return (t * cos + pltpu.roll(t, LANES - MLA_ROPE // 2, 1) * sin_lo
                + pltpu.roll(t, MLA_ROPE // 2, 1) * sin_hi)

    gqn, gkn = gqn_ref[...], gkn_ref[...]
    kr_ss = jnp.sum(kr * kr, axis=-1, keepdims=True)
    kr_rot = rope(kr * gkn[:, MLA_NOPE:])
    for h in range(MLA_HEADS):
        lo = h * MLA_HEAD_PAD
        qh = qa[:, lo:lo + MLA_HEAD_PAD]
        inv = lax.rsqrt(jnp.sum(qh * qh, axis=-1, keepdims=True) * (1.0 / MLA_QK) + EPS) * q_scale
        q_ref[0, h, :, :MLA_NOPE] = ((qh[:, :MLA_NOPE] * gqn[:, :MLA_NOPE]) * inv).astype(BF16)
        q_ref[0, h, :, MLA_NOPE:] = (rope(qh[:, MLA_NOPE:] * gqn[:, MLA_NOPE:]) * inv).astype(BF16)
        kn = kva[:, lo:lo + MLA_NOPE]
        kss = jnp.sum(kn * kn, axis=-1, keepdims=True) + kr_ss
        kinv = lax.rsqrt(kss * (1.0 / MLA_QK) + EPS)
        k_ref[0, h, :, :MLA_NOPE] = ((kn * gkn[:, :MLA_NOPE]) * kinv).astype(BF16)
        k_ref[0, h, :, MLA_NOPE:] = (kr_rot * kinv).astype(BF16)
        vt_ref[0, h] = kva[:, lo + MLA_NOPE:lo + MLA_HEAD_PAD].T.astype(BF16)


def _mla_prep(z, g_q, g_kv, w_uq_p, w_ukv, g_qn_p, g_kn_p, tabs, tm=256):
    b, s, zw = z.shape
    nh = MLA_HEADS
    row = lambda w: pl.BlockSpec((1, w), lambda bi, i: (0, 0))
    full = lambda a: pl.BlockSpec(a.shape, lambda bi, i: (0, 0))
    tab = pl.BlockSpec((1, tm, LANES), lambda bi, i: (bi, i, 0))
    head = lambda w: pl.BlockSpec((1, nh, tm, w), lambda bi, i: (bi, 0, i, 0))
    kern = functools.partial(_mla_prep_kernel, q_scale=MLA_QK ** -0.5 * LOG2E)
    return pl.pallas_call(
        kern,
        name="mla_prep",
        grid=(b, s // tm),
        in_specs=[pl.BlockSpec((1, tm, zw), lambda bi, i: (bi, i, 0)),
                  row(MLA_Q_RANK), row(MLA_KV_RANK), full(w_uq_p), full(w_ukv),
                  row(MLA_HEAD_PAD), row(MLA_HEAD_PAD), tab, tab, tab],
        out_specs=[head(MLA_HEAD_PAD), head(MLA_HEAD_PAD),
                   pl.BlockSpec((1, nh, MLA_V, tm), lambda bi, i: (bi, 0, 0, i))],
        out_shape=[jax.ShapeDtypeStruct((b, nh, s, MLA_HEAD_PAD), BF16),
                   jax.ShapeDtypeStruct((b, nh, s, MLA_HEAD_PAD), BF16),
                   jax.ShapeDtypeStruct((b, nh, MLA_V, s), BF16)],
        compiler_params=_params(("parallel", "parallel")),
    )(z, g_q, g_kv, w_uq_p, w_ukv, g_qn_p, g_kn_p, *tabs)


def _flash_kernel(q_ref, k_ref, vt_ref, o_ref, acc_ref, *, t, nsub):
    qi = pl.program_id(2)
    acc_ref[...] = jnp.zeros(acc_ref.shape, F32)
    qs = [q_ref[0, 0, c * t:(c + 1) * t, :] for c in range(nsub)]

    def scores(c, j, diagonal):
        start = pl.multiple_of(j * t, t)
        k = k_ref[0, 0, pl.ds(start, t), :]
        st = lax.dot_general(k, qs[c], _NT, preferred_element_type=F32)
        if diagonal:
            key = lax.broadcasted_iota(jnp.int32, (t, t), 0)
            qry = lax.broadcasted_iota(jnp.int32, (t, t), 1)
            st = jnp.where(key <= qry, st, NEG_INF)
        return st

    def soft(st, m, l):
        m_new = jnp.maximum(m, jnp.max(st, axis=0, keepdims=True))
        p = jnp.exp2(st - m_new)
        alpha = jnp.exp2(m - m_new)
        l_new = alpha * l + jnp.sum(p, axis=0, keepdims=True)
        return p.astype(BF16), alpha, m_new, l_new

    def accum(c, j, p, alpha):
        start = pl.multiple_of(j * t, t)
        vt = vt_ref[0, 0, :, pl.ds(start, t)]
        acc_ref[c] = alpha * acc_ref[c] + jnp.dot(vt, p, preferred_element_type=F32)

    def multi(chains, j, carry, diag_chain):
        carry = list(carry)
        sts = [scores(c, j, c == diag_chain) for c in chains]
        ps = []
        for c, st in zip(chains, sts):
            p, alpha, carry[2 * c], carry[2 * c + 1] = soft(st, carry[2 * c], carry[2 * c + 1])
            ps.append((p, alpha))
        for c, (p, alpha) in zip(chains, ps):
            accum(c, j, p, alpha)
        return tuple(carry)

    def body(j, carry):
        return multi(range(nsub), j, carry, -1)

    init = []
    for c in range(nsub):
        init += [jnp.full((1, t), NEG_INF, F32), jnp.zeros((1, t), F32)]
    carry = lax.fori_loop(0, nsub * qi, body, tuple(init))
    for d in range(nsub):
        carry = multi(range(d, nsub), nsub * qi + d, carry, d)
    for c in range(nsub):
        ot = acc_ref[c] / carry[2 * c + 1]
        o_ref[0, c * t:(c + 1) * t, :] = ot.T.astype(o_ref.dtype)


def _flash_attention(q, k, vt, t=512, nsub=8):
    b, nh, s, dq = q.shape
    dv = vt.shape[2]
    tq = t * nsub
    return pl.pallas_call(
        functools.partial(_flash_kernel, t=t, nsub=nsub),
        name="mla_flash",
        grid=(b, nh, s // tq),
        in_specs=[pl.BlockSpec((1, 1, tq, dq), lambda bi, h, i: (bi, h, i, 0)),
                  pl.BlockSpec((1, 1, s, dq), lambda bi, h, i: (bi, h, 0, 0)),
                  pl.BlockSpec((1, 1, dv, s), lambda bi, h, i: (bi, h, 0, 0))],
        out_specs=pl.BlockSpec((1, tq, dv), lambda bi, h, i: (bi, i, h)),
        out_shape=jax.ShapeDtypeStruct((b, s, nh * dv), BF16),
        scratch_shapes=[pltpu.VMEM((nsub, dv, t), F32)],
        compiler_params=_params(("parallel", "parallel", "arbitrary")),
    )(q, k, vt)


_DIL_TN = 1024
_DIL_SUB = 256
_DIL_HPT = _DIL_TN // DIL_HEAD_DIM
_DIL_TPP = DIL_HEADS // _DIL_HPT


def _dil_proj_kernel(h_ref, w_ref, gain_ref, c_ref, s_ref, o_ref, *, q_scale):
    part = pl.program_id(3) // _DIL_TPP
    nsub = _DIL_TN // _DIL_SUB
    hps = _DIL_SUB // LANES
    h = h_ref[0]
    is_v = part == 2
    cos, sin = c_ref[0], s_ref[0]
    gain = gain_ref[0]
    post = jnp.where(part == 0, q_scale, 1.0)
    ri = lax.broadcasted_iota(jnp.int32, (_DIL_SUB, _DIL_SUB), 0) // LANES
    ci = lax.broadcasted_iota(jnp.int32, (_DIL_SUB, _DIL_SUB), 1) // LANES
    seg = jnp.where(ri == ci, 1.0, 0.0).astype(BF16)

    def main(sb):
        return jnp.dot(h, w_ref[:, sb * _DIL_SUB:(sb + 1) * _DIL_SUB], preferred_element_type=F32)

    def sumsq(y):
        return jnp.dot((y * y).astype(BF16), seg, preferred_element_type=F32)

    def finish(sb, y, ss):
        inv = lax.rsqrt(ss * (1.0 / DIL_HEAD_DIM) + EPS)
        for hh in range(hps):
            cols = slice(hh * LANES, (hh + 1) * LANES)
            yh = y[:, cols]
            yn = (yh * inv[:, cols]) * gain
            rot = yn * cos + pltpu.roll(yn, DIL_HEAD_DIM // 2, 1) * sin
            o_ref[0, 0, 0, sb * hps + hh] = jnp.where(is_v, yh, rot * post).astype(BF16)

    pairs = [(sb, sb + 1) for sb in range(0, nsub, 2)]
    ys = {sb: main(sb) for sb in pairs[0]}
    for n, pair in enumerate(pairs):
        ss = {sb: sumsq(ys[sb]) for sb in pair}
        if n + 1 < len(pairs):
            ys.update({sb: main(sb) for sb in pairs[n + 1]})
        for sb in pair:
            finish(sb, ys[sb], ss[sb])


def _dil_project(h, w_in, gains, cos_t, sin_t, group, dil):
    b, s, d = h.shape
    l = s // dil
    tl = min(512, l)
    tiles = 3 * _DIL_TPP
    kern = functools.partial(_dil_proj_kernel, q_scale=DIL_HEAD_DIM ** -0.5 * LOG2E)
    return pl.pallas_call(
        kern,
        name=f"dil_proj_d{dil}",
        grid=(b, dil, l // tl, tiles),
        in_specs=[pl.BlockSpec((1, tl, d), lambda bi, r, i, j: (bi, i, r)),
                  pl.BlockSpec((d, _DIL_TN), lambda bi, r, i, j: (0, group * tiles + j)),
                  pl.BlockSpec((1, 1, LANES), lambda bi, r, i, j: (group * 3 + j // _DIL_TPP, 0, 0)),
                  pl.BlockSpec((1, tl, LANES), lambda bi, r, i, j: (bi, i, r)),
                  pl.BlockSpec((1, tl, LANES), lambda bi, r, i, j: (bi, i, r))],
        out_specs=pl.BlockSpec((1, 1, 1, _DIL_HPT, tl, LANES),
                               lambda bi, r, i, j: (bi, r, j // _DIL_TPP, j % _DIL_TPP, i, 0)),
        out_shape=jax.ShapeDtypeStruct((b, dil, 3, DIL_HEADS, l, LANES), BF16),
        compiler_params=_params(("parallel", "parallel", "parallel", "arbitrary")),
    )(h.reshape(b, l, dil * d), w_in, gains,
      cos_t.reshape(b, l, dil * LANES), sin_t.reshape(b, l, dil * LANES))


def _dil_attn_kernel(*refs, tq, has_state, is_last):
    q_ref, kc_ref, kp_ref, vc_ref, vp_ref = refs[:5]
    refs = refs[5:]
    if has_state:
        acc_in, m_in, l_in = refs[:3]
        refs = refs[3:]
    if is_last:
        (o_ref,) = refs
    else:
        acc_out, m_out, l_out = refs

    first = pl.program_id(2) == 0
    row = lax.broadcasted_iota(jnp.int32, (tq, tq), 0)
    col = lax.broadcasted_iota(jnp.int32, (tq, tq), 1)
    dist = row - col
    bias_c = jnp.where(dist >= 0, jnp.where(dist <= DIL_STEPS, 0.0, NEG_INF), NEG_INF)
    rowp = lax.broadcasted_iota(jnp.int32, (tq, DIL_STEPS), 0)
    colp = lax.broadcasted_iota(jnp.int32, (tq, DIL_STEPS), 1)
    bias_p = jnp.where(colp >= rowp, jnp.where(first, NEG_INF, 0.0), NEG_INF)
    lane = lax.broadcasted_iota(jnp.int32, (tq, LANES), 1)
    if has_state:
        m_old, l_old = m_in[0], l_in[0]
    m_tile = jnp.zeros((tq, LANES), F32)
    l_tile = jnp.zeros((tq, LANES), F32)

    for h in range(DIL_HEADS):
        q = q_ref[0, 0, 0, h]
        sc = lax.dot_general(q, kc_ref[0, 0, 0, h], _NT, preferred_element_type=F32) + bias_c
        sp = lax.dot_general(q, kp_ref[0, 0, 0, h], _NT, preferred_element_type=F32) + bias_p
        m_new = jnp.maximum(jnp.max(sc, axis=-1, keepdims=True), jnp.max(sp, axis=-1, keepdims=True))
        if has_state:
            m_prev = m_old[:, h:h + 1]
            m_new = jnp.maximum(m_new, m_prev)
            alpha = jnp.exp2(m_prev - m_new)
        pc = jnp.exp2(sc - m_new)
        pp = jnp.exp2(sp - m_new)
        l_new = jnp.sum(pc, axis=-1, keepdims=True) + jnp.sum(pp, axis=-1, keepdims=True)
        acc = (jnp.dot(pc.astype(BF16), vc_ref[0, 0, 0, h], preferred_element_type=F32)
               + jnp.dot(pp.astype(BF16), vp_ref[0, 0, 0, h], preferred_element_type=F32))
        cols = slice(h * LANES, (h + 1) * LANES)
        if has_state:
            l_new = l_new + alpha * l_old[:, h:h + 1]
            acc = acc + alpha * acc_in[0, :, cols]
        if is_last:
            o_ref[0, :, cols] = (acc / l_new).astype(o_ref.dtype)
        else:
            acc_out[0, :, cols] = acc
            m_tile = jnp.where(lane == h, m_new, m_tile)
            l_tile = jnp.where(lane == h, l_new, l_tile)
    if not is_last:
        m_out[0] = m_tile
        l_out[0] = l_tile


def _dil_attention(qkv, state, is_last, tq=512):
    b, dil, _, nh, l, dh = qkv.shape
    tq = min(tq, l)
    width = nh * dh
    ratio = tq // DIL_STEPS
    cur = lambda part: pl.BlockSpec((1, 1, 1, nh, tq, dh), lambda bi, r, i: (bi, r, part, 0, i, 0))
    prev = lambda part: pl.BlockSpec(
        (1, 1, 1, nh, DIL_STEPS, dh),
        lambda bi, r, i: (bi, r, part, 0, jnp.maximum(i * ratio - 1, 0), 0))
    wide = pl.BlockSpec((1, tq, width), lambda bi, r, i: (bi, i, r))
    thin = pl.BlockSpec((1, tq, LANES), lambda bi, r, i: (bi, i, r))
    in_specs = [cur(0), cur(1), prev(1), cur(2), prev(2)]
    args = [qkv] * 5
    if state is not None:
        in_specs += [wide, thin, thin]
        args += [state[0].reshape(b, l, dil * width), state[1].reshape(b, l, dil * LANES),
                 state[2].reshape(b, l, dil * LANES)]
    if is_last:
        out_specs = wide
        out_shape = jax.ShapeDtypeStruct((b, l, dil * width), BF16)
    else:
        out_specs = [wide, thin, thin]
        out_shape = [jax.ShapeDtypeStruct((b, l, dil * width), F32),
                     jax.ShapeDtypeStruct((b, l, dil * LANES), F32),
                     jax.ShapeDtypeStruct((b, l, dil * LANES), F32)]
    kern = functools.partial(_dil_attn_kernel, tq=tq, has_state=state is not None, is_last=is_last)
    out = pl.pallas_call(
        kern,
        name=f"dil_attn_d{dil}",
        grid=(b, dil, l // tq),
        in_specs=in_specs,
        out_specs=out_specs,
        out_shape=out_shape,
        compiler_params=_params(("parallel", "parallel", "parallel")),
    )(*args)
    s = l * dil
    if is_last:
        return out.reshape(b, s, width)
    return (out[0].reshape(b, s, width), out[1].reshape(b, s, LANES), out[2].reshape(b, s, LANES))


def _batcher_pairs(n):
    pairs = []
    p = 1
    while p < n:
        k = p
        while k >= 1:
            for j in range(k % p, n - k, 2 * k):
                for i in range(min(k, n - j - k)):
                    if (i + j) // (2 * p) == (i + j + k) // (2 * p):
                        pairs.append((i + j, i + j + k))
            k //= 2
        p *= 2
    return pairs


def _sorted_top(vs, count):
    vs = list(vs) + [None] * (count - len(vs))

    def exchange(i, j):
        a, b = vs[i], vs[j]
        if b is None:
            return
        if a is None:
            vs[i], vs[j] = b, None
            return
        vs[i], vs[j] = jnp.maximum(a, b), jnp.minimum(a, b)

    for i, j in _batcher_pairs(count):
        exchange(i, j)
    shape = next(v.shape for v in vs if v is not None)
    vs = [jnp.full(shape, NEG_INF, F32) if v is None else v for v in vs]
    for shift in (4, 2, 1):
        other = [pltpu.roll(v, shift, 0) for v in vs]
        vs = [jnp.maximum(vs[i], other[count - 1 - i]) for i in range(count)]
        k = count // 2
        while k >= 1:
            for i in range(count):
                if i & k == 0:
                    exchange(i, i + k)
            k //= 2
    return vs


def _peer_topk_kernel(q_ref, keys_ref, rk2_ref, c1_ref, e1_ref, e2_ref):
    k = PEER_TOPK
    tt = q_ref.shape[0]
    nslab = PEER_NKEYS // 8
    sub = lax.broadcasted_iota(jnp.int32, (8, tt), 0)

    def by_sublane(vals):
        out = vals[0]
        for i in range(1, 8):
            out = jnp.where(sub == i, vals[i], out)
        return out

    def count(slab, thr):
        return jnp.sum(jnp.where(slab >= thr, 1.0, 0.0), axis=0, keepdims=True)

    for h in range(PEER_HEADS):
        sts, tops = [], []
        for p in range(2):
            hp = 2 * h + p
            qs = q_ref[:, hp * PEER_NKEYS:(hp + 1) * PEER_NKEYS]
            st = lax.dot_general(keys_ref[hp], qs, _NT, preferred_element_type=F32,
                                 precision=lax.Precision.HIGHEST)
            slabs = [st[8 * i:8 * i + 8, :] for i in range(nslab)]
            sts.append(slabs)
            tops.append(_sorted_top(slabs, k))
        a, bv = tops
        b_lo, b_hi, a_hi = by_sublane(bv[:8]), by_sublane(bv[8:]), by_sublane(a[8:])
        cand = [a[0] + b_lo, a[0] + b_hi, a[1] + b_lo]
        for i in range(2, 8):
            cand.append(jnp.where(sub < k // (i + 1), a[i] + b_lo, NEG_INF))
        cand.append(a_hi + bv[0])
        best = _sorted_top(cand, k)
        top, thr = best[0], best[k - 1]
        z = jnp.exp(best[0] - top)
        for r in range(1, k):
            z = z + jnp.exp(best[r] - top)
        cnt = [count(cand[0], thr) + count(cand[1], thr)]
        cnt += [count(cand[i + 1], thr) for i in range(1, 8)]
        tail = jnp.where(cand[9] >= thr, 1.0, 0.0)
        cnt += [tail[i:i + 1] for i in range(8)]
        inv_z = 1.0 / z
        for i in range(nslab):
            s1, s2 = sts[0][i], sts[1][i]
            c1 = jnp.zeros(s1.shape, F32)
            rk = jnp.full(s2.shape, float(k), F32)
            for r in range(k - 1, -1, -1):
                c1 = jnp.where(s1 >= a[r], cnt[r], c1)
                rk = jnp.where(s2 >= bv[r], float(r), rk)
            rows = slice(8 * i, 8 * i + 8)
            c1_ref[h, rows, :] = c1
            rk2_ref[h, rows, :] = rk.astype(BF16)
            e1_ref[h, rows, :] = jnp.exp(s1 - a[0]) * inv_z
            e2_ref[h, rows, :] = jnp.exp(s2 - bv[0]).astype(BF16)


def _peer_topk(q2d, sub_keys, tt=512):
    t = q2d.shape[0]
    nh = PEER_HEADS
    keys = sub_keys.reshape(2 * nh, PEER_NKEYS, sub_keys.shape[-1])
    big = pl.BlockSpec((nh, PEER_NKEYS, tt), lambda i: (0, 0, i))
    return pl.pallas_call(
        _peer_topk_kernel,
        name="peer_topk",
        grid=(t // tt,),
        in_specs=[pl.BlockSpec((tt, q2d.shape[1]), lambda i: (i, 0)),
                  pl.BlockSpec(keys.shape, lambda i: (0, 0, 0))],
        out_specs=[big, big, big, big],
        out_shape=[jax.ShapeDtypeStruct((nh, PEER_NKEYS, t), BF16),
                   jax.ShapeDtypeStruct((nh, PEER_NKEYS, t), F32),
                   jax.ShapeDtypeStruct((nh, PEER_NKEYS, t), F32),
                   jax.ShapeDtypeStruct((nh, PEER_NKEYS, t), BF16)],
        compiler_params=_params(("parallel",)),
    )(q2d, keys)


def _peer_dense_kernel(x_ref, u_ref, vt_ref, rk2_ref, c1_ref, e1_ref, e2_ref, o_ref, act_ref, *, te):
    j = pl.program_id(1)

    @pl.when(j == 0)
    def _():
        o_ref[...] = jnp.zeros(o_ref.shape, F32)

    half = te // 2
    per = half // PEER_NKEYS
    pre = [lax.dot_general(u_ref[sb * half:(sb + 1) * half, :], x_ref[...], _NT,
                           preferred_element_type=F32) for sb in range(2)]
    for sb in range(2):
        a = pre[sb]
        ge = (0.5 * a * (1.0 + lax.erf(a * (1.0 / math.sqrt(2.0))))).astype(BF16)
        for ii in range(per):
            i1 = j * (2 * per) + sb * per + ii
            gate = jnp.zeros((PEER_NKEYS, x_ref.shape[0]), BF16)
            for h in range(PEER_HEADS):
                c1 = c1_ref[h, pl.ds(i1, 1), :].astype(BF16)
                e1 = e1_ref[h, pl.ds(i1, 1), :].astype(BF16)
                gate = gate + jnp.where(rk2_ref[h] < c1, e2_ref[h], jnp.zeros((), BF16)) * e1
            lo = sb * half + ii * PEER_NKEYS
            act_ref[lo:lo + PEER_NKEYS, :] = ge[ii * PEER_NKEYS:(ii + 1) * PEER_NKEYS, :] * gate
    o_ref[...] += jnp.dot(vt_ref[...], act_ref[...], preferred_element_type=F32)


def _peer_dense(h2d, u_bf, vt_bf, rk2, c1, e1, e2, tt=512, te=1024):
    t, d = h2d.shape
    ne = u_bf.shape[0]
    big = pl.BlockSpec((PEER_HEADS, PEER_NKEYS, tt), lambda i, j: (0, 0, i))
    return pl.pallas_call(
        functools.partial(_peer_dense_kernel, te=te),
        name="peer_dense",
        grid=(t // tt, ne // te),
        in_specs=[pl.BlockSpec((tt, d), lambda i, j: (i, 0)),
                  pl.BlockSpec((te, d), lambda i, j: (j, 0)),
                  pl.BlockSpec((d, te), lambda i, j: (0, j)),
                  big, big, big, big],
        out_specs=pl.BlockSpec((d, tt), lambda i, j: (0, i)),
        out_shape=jax.ShapeDtypeStruct((d, t), F32),
        scratch_shapes=[pltpu.VMEM((te, tt), BF16)],
        compiler_params=_params(("parallel", "arbitrary"), vmem_mb=56),
    )(h2d, u_bf, vt_bf, rk2, c1, e1, e2)


def _res_t_kernel(x_ref, yt_ref, g_ref, o_ref):
    o_ref[0] = x_ref[0] + g_ref[0] * yt_ref[...].T


def _residual_t(x, y_t, modr, row_gate, tm=512, tn=512):
    b, s, d = x.shape
    nt = s // tm
    return pl.pallas_call(
        _res_t_kernel,
        name="residual_t",
        grid=(b, nt, d // tn),
        in_specs=[pl.BlockSpec((1, tm, tn), lambda bi, i, j: (bi, i, j)),
                  pl.BlockSpec((tn, tm), lambda bi, i, j: (j, bi * nt + i)),
                  pl.BlockSpec((1, 1, tn), lambda bi, i, j: (row_gate(bi), 0, j))],
        out_specs=pl.BlockSpec((1, tm, tn), lambda bi, i, j: (bi, i, j)),
        out_shape=jax.ShapeDtypeStruct((b, s, d), F32),
        compiler_params=_params(("parallel", "parallel", "parallel")),
    )(x, y_t, modr)


def _pad_cols(a, width):
    return jnp.pad(a, ((0, 0), (0, width - a.shape[1])))


def _mla_weights(w_in, w_uq, g_qn, g_kn):
    w_in_p = _pad_cols(w_in, MLA_Q_RANK + MLA_KV_RANK + LANES).astype(BF16)
    w_uq_p = jnp.pad(w_uq.reshape(MLA_Q_RANK, MLA_HEADS, MLA_QK),
                     ((0, 0), (0, 0), (0, MLA_HEAD_PAD - MLA_QK)))
    w_uq_p = w_uq_p.reshape(MLA_Q_RANK, MLA_HEADS * MLA_HEAD_PAD).astype(BF16)
    pad = lambda g: _pad_cols(g.reshape(1, MLA_QK), MLA_HEAD_PAD)
    return w_in_p, w_uq_p, pad(g_qn), pad(g_kn)


def _peer(x, layer, norm_row, modr, row, w_q, sub_keys, u_tab, v_tab):
    b, s, d = x.shape
    q, hmod = _norm_matmul(x, norm_row, modr, row(layer, 4), row(layer, 3), w_q.astype(BF16), emit_h=True)
    rk2, c1, e1, e2 = _peer_topk(q.reshape(b * s, -1), sub_keys)
    y_t = _peer_dense(hmod.reshape(b * s, d), u_tab.astype(BF16), v_tab.T.astype(BF16), rk2, c1, e1, e2)
    return _residual_t(x, y_t, modr, row(layer, 5))


def kernel(x, c, positions, ada_w, ada_b, norm_g, mla_w_in, mla_g_q, mla_w_uq, mla_g_kv, mla_w_ukv, mla_g_qn, mla_g_kn, mla_w_o, dil_w_in, dil_g_qn, dil_g_kn, dil_w_o, peer_w_q, peer_sub_keys, peer_u, peer_v):
    b, s, d = x.shape
    depth = ada_w.shape[0]
    mod = _ada_mod(c, ada_w, ada_b)
    modr = mod.reshape(depth * b * 6, 1, d)
    row = lambda layer, part: (lambda bi: (layer * b + bi) * 6 + part)
    m_cos, m_sin_lo, m_sin_hi, d_cos, d_sin = _rope_tables(positions)

    for layer in range(depth):
        g1 = norm_g[layer, 0].reshape(1, d)
        g2 = norm_g[layer, 1].reshape(1, d)
        a = layer // 2
        if layer % 2 == 0:
            w_in_p, w_uq_p, g_qn_p, g_kn_p = _mla_weights(mla_w_in[a], mla_w_uq[a], mla_g_qn[a], mla_g_kn[a])
            (z,) = _norm_matmul(x, g1, modr, row(layer, 1), row(layer, 0), w_in_p, emit_h=False)
            q, k, v = _mla_prep(z, mla_g_q[a].reshape(1, -1), mla_g_kv[a].reshape(1, -1), w_uq_p,
                                mla_w_ukv[a].astype(BF16), g_qn_p, g_kn_p, (m_cos, m_sin_lo, m_sin_hi))
            o = _flash_attention(q, k, v)
            x = _proj_residual(o, mla_w_o[a].astype(BF16), x, modr, row(layer, 2))
        else:
            hmod = _norm_mod(x, g1, modr, row(layer, 1), row(layer, 0))
            w_in = dil_w_in[a].astype(BF16)
            ones = jnp.ones((len(DIL_GROUPS), DIL_HEAD_DIM), F32)
            gains = jnp.stack([dil_g_qn[a], dil_g_kn[a], ones], axis=1).reshape(-1, 1, DIL_HEAD_DIM)
            state = None
            for gi, (_, dil) in enumerate(DIL_GROUPS):
                qkv = _dil_project(hmod, w_in, gains, d_cos, d_sin, gi, dil)
                state = _dil_attention(qkv, state, is_last=gi == len(DIL_GROUPS) - 1)
            x = _proj_residual(state, dil_w_o[a].astype(BF16), x, modr, row(layer, 2))
        x = _peer(x, layer, g2, modr, row, peer_w_q[layer], peer_sub_keys[layer], peer_u[layer], peer_v[layer])
    return x
```

```python
import functools
import math

import jax
import jax.numpy as jnp
from jax import lax
from jax.experimental import pallas as pl
from jax.experimental.pallas import tpu as pltpu

F32 = jnp.float32
BF16 = jnp.bfloat16
EPS = 1e-6
ROPE_THETA = 10000.0
LOG2E = 1.4426950408889634
NEG_INF = float("-inf")
LANES = 128

MLA_HEADS = 16
MLA_Q_RANK = 512
MLA_KV_RANK = 512
MLA_NOPE = 128
MLA_ROPE = 64
MLA_QK = MLA_NOPE + MLA_ROPE
MLA_V = 128
MLA_HEAD_PAD = 256

DIL_GROUPS = ((128, 1), (512, 4), (2048, 16))
DIL_HEADS = 16
DIL_HEAD_DIM = 128
DIL_STEPS = 128

PEER_HEADS = 8
PEER_NKEYS = 128
PEER_TOPK = 16

_NT = (((1,), (1,)), ((), ()))


def _params(sem, vmem_mb=48):
    return pltpu.CompilerParams(dimension_semantics=sem, vmem_limit_bytes=vmem_mb << 20)


def _ada_kernel(c_ref, w_ref, b_ref, o_ref):
    c = c_ref[...]
    sc = c / (1.0 + jnp.exp(-c))
    o_ref[0] = jnp.dot(sc, w_ref[0], preferred_element_type=F32,
                       precision=lax.Precision.HIGHEST) + b_ref[0]


def _ada_mod(c, ada_w, ada_b, tn=768):
    depth, d, n = ada_w.shape
    b = c.shape[0]
    return pl.pallas_call(
        _ada_kernel,
        name="ada_mod",
        grid=(depth, n // tn),
        in_specs=[pl.BlockSpec((b, d), lambda l, j: (0, 0)),
                  pl.BlockSpec((1, d, tn), lambda l, j: (l, 0, j)),
                  pl.BlockSpec((1, 1, tn), lambda l, j: (l, 0, j))],
        out_specs=pl.BlockSpec((1, b, tn), lambda l, j: (l, 0, j)),
        out_shape=jax.ShapeDtypeStruct((depth, b, n), F32),
        compiler_params=_params(("parallel", "parallel")),
    )(c, ada_w, ada_b.reshape(depth, 1, n))


def _rope_tab_kernel(pos_ref, fm_ref, fd_ref, mc_ref, ms1_ref, ms2_ref, dc_ref, ds_ref):
    pos = pos_ref[0].astype(F32)
    lane = lax.broadcasted_iota(jnp.int32, (pos.shape[0], LANES), 1)
    am = pos * fm_ref[...]
    cm, sm = jnp.cos(am), jnp.sin(am)
    mc_ref[0] = jnp.where(lane < 2 * (MLA_ROPE // 2), cm, 0.0)
    ms1_ref[0] = jnp.where(lane < MLA_ROPE // 2, -sm, 0.0)
    ms2_ref[0] = jnp.where(lane < MLA_ROPE // 2, 0.0, jnp.where(lane < MLA_ROPE, sm, 0.0))
    ad = pos * fd_ref[...]
    dc_ref[0] = jnp.cos(ad)
    sd = jnp.sin(ad)
    ds_ref[0] = jnp.where(lane < DIL_HEAD_DIM // 2, -sd, sd)


def _rope_tables(positions, ts=512):
    b, s = positions.shape
    hm, hd = MLA_ROPE // 2, DIL_HEAD_DIM // 2
    inv_m = ROPE_THETA ** (-jnp.arange(hm, dtype=F32) / hm)
    inv_d = ROPE_THETA ** (-jnp.arange(hd, dtype=F32) / hd)
    fm = jnp.concatenate([inv_m, inv_m, jnp.zeros((LANES - 2 * hm,), F32)]).reshape(1, LANES)
    fd = jnp.concatenate([inv_d, inv_d]).reshape(1, LANES)
    tab = jax.ShapeDtypeStruct((b, s, LANES), F32)
    row = pl.BlockSpec((1, ts, LANES), lambda bi, i: (bi, i, 0))
    frq = pl.BlockSpec((1, LANES), lambda bi, i: (0, 0))
    return pl.pallas_call(
        _rope_tab_kernel,
        name="rope_tables",
        grid=(b, s // ts),
        in_specs=[pl.BlockSpec((1, ts, 1), lambda bi, i: (bi, i, 0)), frq, frq],
        out_specs=[row] * 5,
        out_shape=[tab] * 5,
        compiler_params=_params(("parallel", "parallel")),
    )(positions.reshape(b, s, 1), fm, fd)


def _norm_matmul_kernel(x_ref, g_ref, sc_ref, sh_ref, w_ref, o_ref, *h_out):
    x = x_ref[0]
    inv = lax.rsqrt(jnp.mean(x * x, axis=-1, keepdims=True) + EPS)
    h = (x * inv) * g_ref[...]
    h = (h * (1.0 + sc_ref[0]) + sh_ref[0]).astype(BF16)
    for h_ref in h_out:
        h_ref[0] = h
    o_ref[0] = jnp.dot(h, w_ref[...], preferred_element_type=F32)


def _norm_mod_kernel(x_ref, g_ref, sc_ref, sh_ref, o_ref):
    x = x_ref[0]
    inv = lax.rsqrt(jnp.mean(x * x, axis=-1, keepdims=True) + EPS)
    h = (x * inv) * g_ref[...]
    o_ref[0] = (h * (1.0 + sc_ref[0]) + sh_ref[0]).astype(o_ref.dtype)


def _norm_mod(x, g_row, modr, row_scale, row_shift, tm=512):
    b, s, d = x.shape
    return pl.pallas_call(
        _norm_mod_kernel,
        name="norm_mod",
        grid=(b, s // tm),
        in_specs=[pl.BlockSpec((1, tm, d), lambda bi, i: (bi, i, 0)),
                  pl.BlockSpec((1, d), lambda bi, i: (0, 0)),
                  pl.BlockSpec((1, 1, d), lambda bi, i: (row_scale(bi), 0, 0)),
                  pl.BlockSpec((1, 1, d), lambda bi, i: (row_shift(bi), 0, 0))],
        out_specs=pl.BlockSpec((1, tm, d), lambda bi, i: (bi, i, 0)),
        out_shape=jax.ShapeDtypeStruct((b, s, d), BF16),
        compiler_params=_params(("parallel", "parallel")),
    )(x, g_row, modr, modr)


def _norm_matmul(x, g_row, modr, row_scale, row_shift, w, emit_h, tm=512):
    b, s, d = x.shape
    n = w.shape[1]
    rows = lambda width: pl.BlockSpec((1, tm, width), lambda bi, i: (bi, i, 0))
    out_specs, out_shape = [rows(n)], [jax.ShapeDtypeStruct((b, s, n), F32)]
    if emit_h:
        out_specs.append(rows(d))
        out_shape.append(jax.ShapeDtypeStruct((b, s, d), BF16))
    return pl.pallas_call(
        _norm_matmul_kernel,
        name="norm_matmul",
        grid=(b, s // tm),
        in_specs=[rows(d),
                  pl.BlockSpec((1, d), lambda bi, i: (0, 0)),
                  pl.BlockSpec((1, 1, d), lambda bi, i: (row_scale(bi), 0, 0)),
                  pl.BlockSpec((1, 1, d), lambda bi, i: (row_shift(bi), 0, 0)),
                  pl.BlockSpec((d, n), lambda bi, i: (0, 0))],
        out_specs=out_specs,
        out_shape=out_shape,
        compiler_params=_params(("parallel", "parallel")),
    )(x, g_row, modr, modr, w)


def _proj_res_kernel(a_ref, w_ref, x_ref, g_ref, o_ref):
    y = jnp.dot(a_ref[0], w_ref[...], preferred_element_type=F32)
    o_ref[0] = x_ref[0] + g_ref[0] * y


def _proj_residual(a, w, x, modr, row_gate, tm=512):
    b, s, k = a.shape
    d = w.shape[1]
    tn = d
    return pl.pallas_call(
        _proj_res_kernel,
        name="proj_residual",
        grid=(b, s // tm, d // tn),
        in_specs=[pl.BlockSpec((1, tm, k), lambda bi, i, j: (bi, i, 0)),
                  pl.BlockSpec((k, tn), lambda bi, i, j: (0, j)),
                  pl.BlockSpec((1, tm, tn), lambda bi, i, j: (bi, i, j)),
                  pl.BlockSpec((1, 1, tn), lambda bi, i, j: (row_gate(bi), 0, j))],
        out_specs=pl.BlockSpec((1, tm, tn), lambda bi, i, j: (bi, i, j)),
        out_shape=jax.ShapeDtypeStruct((b, s, d), F32),
        compiler_params=_params(("parallel", "parallel", "parallel")),
    )(a, w, x, modr)


def _mla_prep_kernel(z_ref, gq_ref, gkv_ref, wuq_ref, wukv_ref, gqn_ref, gkn_ref,
                     c_ref, s1_ref, s2_ref, q_ref, k_ref, vt_ref, *, q_scale):
    z = z_ref[0]
    cq = z[:, :MLA_Q_RANK]
    ckv = z[:, MLA_Q_RANK:MLA_Q_RANK + MLA_KV_RANK]
    kr = z[:, MLA_Q_RANK + MLA_KV_RANK:]

    def rms(t, g):
        return (t * lax.rsqrt(jnp.mean(t * t, axis=-1, keepdims=True) + EPS)) * g

    qa = jnp.dot(rms(cq, gq_ref[...]).astype(BF16), wuq_ref[...], preferred_element_type=F32)
    kva = jnp.dot(rms(ckv, gkv_ref[...]).astype(BF16), wukv_ref[...], preferred_element_type=F32)
    cos, sin_lo, sin_hi = c_ref[0], s1_ref[0], s2_ref[0]

    def rope(t):
        return (t * cos + pltpu.roll(t, LANES - MLA_ROPE // 2, 1) * sin_lo
                + pltpu.roll(t, MLA_ROPE // 2, 1) * sin_hi)

    gqn, gkn = gqn_ref[...], gkn_ref[...]
    kr_ss = jnp.sum(kr * kr, axis=-1, keepdims=True)
    kr_rot = rope(kr * gkn[:, MLA_NOPE:])
    for h in range(MLA_HEADS):
        lo = h * MLA_HEAD_PAD
        qh = qa[:, lo:lo + MLA_HEAD_PAD]
        inv = lax.rsqrt(jnp.sum(qh * qh, axis=-1, keepdims=True) * (1.0 / MLA_QK) + EPS) * q_scale
        q_ref[0, h, :, :MLA_NOPE] = ((qh[:, :MLA_NOPE] * gqn[:, :MLA_NOPE]) * inv).astype(BF16)
        q_ref[0, h, :, MLA_NOPE:] = (rope(qh[:, MLA_NOPE:] * gqn[:, MLA_NOPE:]) * inv).astype(BF16)
        kn = kva[:, lo:lo + MLA_NOPE]
        kss = jnp.sum(kn * kn, axis=-1, keepdims=True) + kr_ss
        kinv = lax.rsqrt(kss * (1.0 / MLA_QK) + EPS)
        k_ref[0, h, :, :MLA_NOPE] = ((kn * gkn[:, :MLA_NOPE]) * kinv).astype(BF16)
        k_ref[0, h, :, MLA_NOPE:] = (kr_rot * kinv).astype(BF16)
        vt_ref[0, h] = kva[:, lo + MLA_NOPE:lo + MLA_HEAD_PAD].T.astype(BF16)


def _mla_prep(z, g_q, g_kv, w_uq_p, w_ukv, g_qn_p, g_kn_p, tabs, tm=256):
    b, s, zw = z.shape
    nh = MLA_HEADS
    row = lambda w: pl.BlockSpec((1, w), lambda bi, i: (0, 0))
    full = lambda a: pl.BlockSpec(a.shape, lambda bi, i: (0, 0))
    tab = pl.BlockSpec((1, tm, LANES), lambda bi, i: (bi, i, 0))
    head = lambda w: pl.BlockSpec((1, nh, tm, w), lambda bi, i: (bi, 0, i, 0))
    kern = functools.partial(_mla_prep_kernel, q_scale=MLA_QK ** -0.5 * LOG2E)
    return pl.pallas_call(
        kern,
        name="mla_prep",
        grid=(b, s // tm),
        in_specs=[pl.BlockSpec((1, tm, zw), lambda bi, i: (bi, i, 0)),
                  row(MLA_Q_RANK), row(MLA_KV_RANK), full(w_uq_p), full(w_ukv),
                  row(MLA_HEAD_PAD), row(MLA_HEAD_PAD), tab, tab, tab],
        out_specs=[head(MLA_HEAD_PAD), head(MLA_HEAD_PAD),
                   pl.BlockSpec((1, nh, MLA_V, tm), lambda bi, i: (bi, 0, 0, i))],
        out_shape=[jax.ShapeDtypeStruct((b, nh, s, MLA_HEAD_PAD), BF16),
                   jax.ShapeDtypeStruct((b, nh, s, MLA_HEAD_PAD), BF16),
                   jax.ShapeDtypeStruct((b, nh, MLA_V, s), BF16)],
        compiler_params=_params(("parallel", "parallel")),
    )(z, g_q, g_kv, w_uq_p, w_ukv, g_qn_p, g_kn_p, *tabs)


def _flash_kernel(q_ref, k_ref, vt_ref, o_ref, acc_ref, *, t, nsub):
    qi = pl.program_id(2)
    acc_ref[...] = jnp.zeros(acc_ref.shape, F32)
    qs = [q_ref[0, 0, c * t:(c + 1) * t, :] for c in range(nsub)]

    def scores(c, j, diagonal):
        start = pl.multiple_of(j * t, t)
        k = k_ref[0, 0, pl.ds(start, t), :]
        st = lax.dot_general(k, qs[c], _NT, preferred_element_type=F32)
        if diagonal:
            key = lax.broadcasted_iota(jnp.int32, (t, t), 0)
            qry = lax.broadcasted_iota(jnp.int32, (t, t), 1)
            st = jnp.where(key <= qry, st, NEG_INF)
        return st

    def soft(st, m, l):
        m_new = jnp.maximum(m, jnp.max(st, axis=0, keepdims=True))
        p = jnp.exp2(st - m_new)
        alpha = jnp.exp2(m - m_new)
        l_new = alpha * l + jnp.sum(p, axis=0, keepdims=True)
        return p.astype(BF16), alpha, m_new, l_new

    def accum(c, j, p, alpha):
        start = pl.multiple_of(j * t, t)
        vt = vt_ref[0, 0, :, pl.ds(start, t)]
        acc_ref[c] = alpha * acc_ref[c] + jnp.dot(vt, p, preferred_element_type=F32)

    def multi(chains, j, carry, diag_chain):
        carry = list(carry)
        sts = [scores(c, j, c == diag_chain) for c in chains]
        ps = []
        for c, st in zip(chains, sts):
            p, alpha, carry[2 * c], carry[2 * c + 1] = soft(st, carry[2 * c], carry[2 * c + 1])
            ps.append((p, alpha))
        for c, (p, alpha) in zip(chains, ps):
            accum(c, j, p, alpha)
        return tuple(carry)

    def body(j, carry):
        return multi(range(nsub), j, carry, -1)

    init = []
    for c in range(nsub):
        init += [jnp.full((1, t), NEG_INF, F32), jnp.zeros((1, t), F32)]
    carry = lax.fori_loop(0, nsub * qi, body, tuple(init))
    for d in range(nsub):
        carry = multi(range(d, nsub), nsub * qi + d, carry, d)
    for c in range(nsub):
        ot = acc_ref[c] / carry[2 * c + 1]
        o_ref[0, c * t:(c + 1) * t, :] = ot.T.astype(o_ref.dtype)


def _flash_attention(q, k, vt, t=512, nsub=8):
    b, nh, s, dq = q.shape
    dv = vt.shape[2]
    tq = t * nsub
    return pl.pallas_call(
        functools.partial(_flash_kernel, t=t, nsub=nsub),
        name="mla_flash",
        grid=(b, nh, s // tq),
        in_specs=[pl.BlockSpec((1, 1, tq, dq), lambda bi, h, i: (bi, h, i, 0)),
                  pl.BlockSpec((1, 1, s, dq), lambda bi, h, i: (bi, h, 0, 0)),
                  pl.BlockSpec((1, 1, dv, s), lambda bi, h, i: (bi, h, 0, 0))],
        out_specs=pl.BlockSpec((1, tq, dv), lambda bi, h, i: (bi, i, h)),
        out_shape=jax.ShapeDtypeStruct((b, s, nh * dv), BF16),
        scratch_shapes=[pltpu.VMEM((nsub, dv, t), F32)],
        compiler_params=_params(("parallel", "parallel", "arbitrary")),
    )(q, k, vt)


_DIL_TN = 1024
_DIL_SUB = 256
_DIL_HPT = _DIL_TN // DIL_HEAD_DIM
_DIL_TPP = DIL_HEADS // _DIL_HPT


def _dil_proj_kernel(h_ref, w_ref, gain_ref, c_ref, s_ref, o_ref, *, q_scale):
    part = pl.program_id(3) // _DIL_TPP
    nsub = _DIL_TN // _DIL_SUB
    hps = _DIL_SUB // LANES
    h = h_ref[0]
    is_v = part == 2
    cos, sin = c_ref[0], s_ref[0]
    gain = gain_ref[0]
    post = jnp.where(part == 0, q_scale, 1.0)
    ri = lax.broadcasted_iota(jnp.int32, (_DIL_SUB, _DIL_SUB), 0) // LANES
    ci = lax.broadcasted_iota(jnp.int32, (_DIL_SUB, _DIL_SUB), 1) // LANES
    seg = jnp.where(ri == ci, 1.0, 0.0).astype(BF16)

    def main(sb):
        return jnp.dot(h, w_ref[:, sb * _DIL_SUB:(sb + 1) * _DIL_SUB], preferred_element_type=F32)

    def sumsq(y):
        return jnp.dot((y * y).astype(BF16), seg, preferred_element_type=F32)

    def finish(sb, y, ss):
        inv = lax.rsqrt(ss * (1.0 / DIL_HEAD_DIM) + EPS)
        for hh in range(hps):
            cols = slice(hh * LANES, (hh + 1) * LANES)
            yh = y[:, cols]
            yn = (yh * inv[:, cols]) * gain
            rot = yn * cos + pltpu.roll(yn, DIL_HEAD_DIM // 2, 1) * sin
            o_ref[0, 0, 0, sb * hps + hh] = jnp.where(is_v, yh, rot * post).astype(BF16)

    pairs = [(sb, sb + 1) for sb in range(0, nsub, 2)]
    ys = {sb: main(sb) for sb in pairs[0]}
    for n, pair in enumerate(pairs):
        ss = {sb: sumsq(ys[sb]) for sb in pair}
        if n + 1 < len(pairs):
            ys.update({sb: main(sb) for sb in pairs[n + 1]})
        for sb in pair:
            finish(sb, ys[sb], ss[sb])


def _dil_project(h, w_in, gains, cos_t, sin_t, group, dil):
    b, s, d = h.shape
    l = s // dil
    tl = min(1024, l)
    tiles = 3 * _DIL_TPP
    kern = functools.partial(_dil_proj_kernel, q_scale=DIL_HEAD_DIM ** -0.5 * LOG2E)
    return pl.pallas_call(
        kern,
        name=f"dil_proj_d{dil}",
        grid=(b, dil, l // tl, tiles),
        in_specs=[pl.BlockSpec((1, tl, d), lambda bi, r, i, j: (bi, i, r)),
                  pl.BlockSpec((d, _DIL_TN), lambda bi, r, i, j: (0, group * tiles + j)),
                  pl.BlockSpec((1, 1, LANES), lambda bi, r, i, j: (group * 3 + j // _DIL_TPP, 0, 0)),
                  pl.BlockSpec((1, tl, LANES), lambda bi, r, i, j: (bi, i, r)),
                  pl.BlockSpec((1, tl, LANES), lambda bi, r, i, j: (bi, i, r))],
        out_specs=pl.BlockSpec((1, 1, 1, _DIL_HPT, tl, LANES),
                               lambda bi, r, i, j: (bi, r, j // _DIL_TPP, j % _DIL_TPP, i, 0)),
        out_shape=jax.ShapeDtypeStruct((b, dil, 3, DIL_HEADS, l, LANES), BF16),
        compiler_params=_params(("parallel", "parallel", "parallel", "arbitrary")),
    )(h.reshape(b, l, dil * d), w_in, gains,
      cos_t.reshape(b, l, dil * LANES), sin_t.reshape(b, l, dil * LANES))


def _dil_attn_kernel(*refs, tq, has_state, is_last):
    q_ref, kc_ref, kp_ref, vc_ref, vp_ref = refs[:5]
    refs = refs[5:]
    if has_state:
        acc_in, m_in, l_in = refs[:3]
        refs = refs[3:]
    if is_last:
        (o_ref,) = refs
    else:
        acc_out, m_out, l_out = refs

    first = pl.program_id(2) == 0
    row = lax.broadcasted_iota(jnp.int32, (tq, tq), 0)
    col = lax.broadcasted_iota(jnp.int32, (tq, tq), 1)
    dist = row - col
    bias_c = jnp.where(dist >= 0, jnp.where(dist <= DIL_STEPS, 0.0, NEG_INF), NEG_INF)
    rowp = lax.broadcasted_iota(jnp.int32, (tq, DIL_STEPS), 0)
    colp = lax.broadcasted_iota(jnp.int32, (tq, DIL_STEPS), 1)
    bias_p = jnp.where(colp >= rowp, jnp.where(first, NEG_INF, 0.0), NEG_INF)
    lane = lax.broadcasted_iota(jnp.int32, (tq, LANES), 1)
    if has_state:
        m_old, l_old = m_in[0], l_in[0]
    m_tile = jnp.zeros((tq, LANES), F32)
    l_tile = jnp.zeros((tq, LANES), F32)

    for h in range(DIL_HEADS):
        q = q_ref[0, 0, 0, h]
        sc = lax.dot_general(q, kc_ref[0, 0, 0, h], _NT, preferred_element_type=F32) + bias_c
        sp = lax.dot_general(q, kp_ref[0, 0, 0, h], _NT, preferred_element_type=F32) + bias_p
        m_new = jnp.maximum(jnp.max(sc, axis=-1, keepdims=True), jnp.max(sp, axis=-1, keepdims=True))
        if has_state:
            m_prev = m_old[:, h:h + 1]
            m_new = jnp.maximum(m_new, m_prev)
            alpha = jnp.exp2(m_prev - m_new)
        pc = jnp.exp2(sc - m_new)
        pp = jnp.exp2(sp - m_new)
        l_new = jnp.sum(pc, axis=-1, keepdims=True) + jnp.sum(pp, axis=-1, keepdims=True)
        acc = (jnp.dot(pc.astype(BF16), vc_ref[0, 0, 0, h], preferred_element_type=F32)
               + jnp.dot(pp.astype(BF16), vp_ref[0, 0, 0, h], preferred_element_type=F32))
        cols = slice(h * LANES, (h + 1) * LANES)
        if has_state:
            l_new = l_new + alpha * l_old[:, h:h + 1]
            acc = acc + alpha * acc_in[0, :, cols]
        if is_last:
            o_ref[0, :, cols] = (acc / l_new).astype(o_ref.dtype)
        else:
            acc_out[0, :, cols] = acc
            m_tile = jnp.where(lane == h, m_new, m_tile)
            l_tile = jnp.where(lane == h, l_new, l_tile)
    if not is_last:
        m_out[0] = m_tile
        l_out[0] = l_tile


def _dil_attention(qkv, state, is_last, tq=512):
    b, dil, _, nh, l, dh = qkv.shape
    tq = min(tq, l)
    width = nh * dh
    ratio = tq // DIL_STEPS
    cur = lambda part: pl.BlockSpec((1, 1, 1, nh, tq, dh), lambda bi, r, i: (bi, r, part, 0, i, 0))
    prev = lambda part: pl.BlockSpec(
        (1, 1, 1, nh, DIL_STEPS, dh),
        lambda bi, r, i: (bi, r, part, 0, jnp.maximum(i * ratio - 1, 0), 0))
    wide = pl.BlockSpec((1, tq, width), lambda bi, r, i: (bi, i, r))
    thin = pl.BlockSpec((1, tq, LANES), lambda bi, r, i: (bi, i, r))
    in_specs = [cur(0), cur(1), prev(1), cur(2), prev(2)]
    args = [qkv] * 5
    if state is not None:
        in_specs += [wide, thin, thin]
        args += [state[0].reshape(b, l, dil * width), state[1].reshape(b, l, dil * LANES),
                 state[2].reshape(b, l, dil * LANES)]
    if is_last:
        out_specs = wide
        out_shape = jax.ShapeDtypeStruct((b, l, dil * width), BF16)
    else:
        out_specs = [wide, thin, thin]
        out_shape = [jax.ShapeDtypeStruct((b, l, dil * width), F32),
                     jax.ShapeDtypeStruct((b, l, dil * LANES), F32),
                     jax.ShapeDtypeStruct((b, l, dil * LANES), F32)]
    kern = functools.partial(_dil_attn_kernel, tq=tq, has_state=state is not None, is_last=is_last)
    out = pl.pallas_call(
        kern,
        name=f"dil_attn_d{dil}",
        grid=(b, dil, l // tq),
        in_specs=in_specs,
        out_specs=out_specs,
        out_shape=out_shape,
        compiler_params=_params(("parallel", "parallel", "parallel")),
    )(*args)
    s = l * dil
    if is_last:
        return out.reshape(b, s, width)
    return (out[0].reshape(b, s, width), out[1].reshape(b, s, LANES), out[2].reshape(b, s, LANES))


def _batcher_pairs(n):
    pairs = []
    p = 1
    while p < n:
        k = p
        while k >= 1:
            for j in range(k % p, n - k, 2 * k):
                for i in range(min(k, n - j - k)):
                    if (i + j) // (2 * p) == (i + j + k) // (2 * p):
                        pairs.append((i + j, i + j + k))
            k //= 2
        p *= 2
    return pairs


def _sorted_top(vs, count):
    vs = list(vs) + [None] * (count - len(vs))

    def exchange(i, j):
        a, b = vs[i], vs[j]
        if b is None:
            return
        if a is None:
            vs[i], vs[j] = b, None
            return
        vs[i], vs[j] = jnp.maximum(a, b), jnp.minimum(a, b)

    for i, j in _batcher_pairs(count):
        exchange(i, j)
    shape = next(v.shape for v in vs if v is not None)
    vs = [jnp.full(shape, NEG_INF, F32) if v is None else v for v in vs]
    for shift in (4, 2, 1):
        other = [pltpu.roll(v, shift, 0) for v in vs]
        vs = [jnp.maximum(vs[i], other[count - 1 - i]) for i in range(count)]
        k = count // 2
        while k >= 1:
            for i in range(count):
                if i & k == 0:
                    exchange(i, i + k)
            k //= 2
    return vs


def _peer_topk_kernel(q_ref, keys_ref, rk2_ref, c1_ref, e1_ref, e2_ref):
    k = PEER_TOPK
    tt = q_ref.shape[0]
    nslab = PEER_NKEYS // 8
    sub = lax.broadcasted_iota(jnp.int32, (8, tt), 0)

    def by_sublane(vals):
        out = vals[0]
        for i in range(1, 8):
            out = jnp.where(sub == i, vals[i], out)
        return out

    def count(slab, thr):
        return jnp.sum(jnp.where(slab >= thr, 1.0, 0.0), axis=0, keepdims=True)

    for h in range(PEER_HEADS):
        sts, tops = [], []
        for p in range(2):
            hp = 2 * h + p
            qs = q_ref[:, hp * PEER_NKEYS:(hp + 1) * PEER_NKEYS]
            st = lax.dot_general(keys_ref[hp], qs, _NT, preferred_element_type=F32,
                                 precision=lax.Precision.HIGHEST)
            slabs = [st[8 * i:8 * i + 8, :] for i in range(nslab)]
            sts.append(slabs)
            tops.append(_sorted_top(slabs, k))
        a, bv = tops
        b_lo, b_hi, a_hi = by_sublane(bv[:8]), by_sublane(bv[8:]), by_sublane(a[8:])
        cand = [a[0] + b_lo, a[0] + b_hi, a[1] + b_lo]
        for i in range(2, 8):
            cand.append(jnp.where(sub < k // (i + 1), a[i] + b_lo, NEG_INF))
        cand.append(a_hi + bv[0])
        best = _sorted_top(cand, k)
        top, thr = best[0], best[k - 1]
        z = jnp.exp(best[0] - top)
        for r in range(1, k):
            z = z + jnp.exp(best[r] - top)
        cnt = [count(cand[0], thr) + count(cand[1], thr)]
        cnt += [count(cand[i + 1], thr) for i in range(1, 8)]
        tail = jnp.where(cand[9] >= thr, 1.0, 0.0)
        cnt += [tail[i:i + 1] for i in range(8)]
        inv_z = 1.0 / z
        for i in range(nslab):
            s1, s2 = sts[0][i], sts[1][i]
            c1 = jnp.zeros(s1.shape, F32)
            rk = jnp.full(s2.shape, float(k), F32)
            for r in range(k - 1, -1, -1):
                c1 = jnp.where(s1 >= a[r], cnt[r], c1)
                rk = jnp.where(s2 >= bv[r], float(r), rk)
            rows = slice(8 * i, 8 * i + 8)
            c1_ref[h, rows, :] = c1
            rk2_ref[h, rows, :] = rk.astype(BF16)
            e1_ref[h, rows, :] = jnp.exp(s1 - a[0]) * inv_z
            e2_ref[h, rows, :] = jnp.exp(s2 - bv[0]).astype(BF16)


def _peer_topk(q2d, sub_keys, tt=512):
    t = q2d.shape[0]
    nh = PEER_HEADS
    keys = sub_keys.reshape(2 * nh, PEER_NKEYS, sub_keys.shape[-1])
    big = pl.BlockSpec((nh, PEER_NKEYS, tt), lambda i: (0, 0, i))
    return pl.pallas_call(
        _peer_topk_kernel,
        name="peer_topk",
        grid=(t // tt,),
        in_specs=[pl.BlockSpec((tt, q2d.shape[1]), lambda i: (i, 0)),
                  pl.BlockSpec(keys.shape, lambda i: (0, 0, 0))],
        out_specs=[big, big, big, big],
        out_shape=[jax.ShapeDtypeStruct((nh, PEER_NKEYS, t), BF16),
                   jax.ShapeDtypeStruct((nh, PEER_NKEYS, t), F32),
                   jax.ShapeDtypeStruct((nh, PEER_NKEYS, t), F32),
                   jax.ShapeDtypeStruct((nh, PEER_NKEYS, t), BF16)],
        compiler_params=_params(("parallel",)),
    )(q2d, keys)


def _peer_dense_kernel(x_ref, u_ref, vt_ref, rk2_ref, c1_ref, e1_ref, e2_ref, o_ref, act_ref, *, te):
    j = pl.program_id(1)

    @pl.when(j == 0)
    def _():
        o_ref[...] = jnp.zeros(o_ref.shape, F32)

    half = te // 2
    per = half // PEER_NKEYS
    pre = [lax.dot_general(u_ref[sb * half:(sb + 1) * half, :], x_ref[...], _NT,
                           preferred_element_type=F32) for sb in range(2)]
    for sb in range(2):
        a = pre[sb]
        ge = (0.5 * a * (1.0 + lax.erf(a * (1.0 / math.sqrt(2.0))))).astype(BF16)
        for ii in range(per):
            i1 = j * (2 * per) + sb * per + ii
            gate = jnp.zeros((PEER_NKEYS, x_ref.shape[0]), BF16)
            for h in range(PEER_HEADS):
                c1 = c1_ref[h, pl.ds(i1, 1), :].astype(BF16)
                e1 = e1_ref[h, pl.ds(i1, 1), :].astype(BF16)
                gate = gate + jnp.where(rk2_ref[h] < c1, e2_ref[h], jnp.zeros((), BF16)) * e1
            lo = sb * half + ii * PEER_NKEYS
            act_ref[lo:lo + PEER_NKEYS, :] = ge[ii * PEER_NKEYS:(ii + 1) * PEER_NKEYS, :] * gate
    o_ref[...] += jnp.dot(vt_ref[...], act_ref[...], preferred_element_type=F32)


def _peer_dense(h2d, u_bf, vt_bf, rk2, c1, e1, e2, tt=512, te=1024):
    t, d = h2d.shape
    ne = u_bf.shape[0]
    big = pl.BlockSpec((PEER_HEADS, PEER_NKEYS, tt), lambda i, j: (0, 0, i))
    return pl.pallas_call(
        functools.partial(_peer_dense_kernel, te=te),
        name="peer_dense",
        grid=(t // tt, ne // te),
        in_specs=[pl.BlockSpec((tt, d), lambda i, j: (i, 0)),
                  pl.BlockSpec((te, d), lambda i, j: (j, 0)),
                  pl.BlockSpec((d, te), lambda i, j: (0, j)),
                  big, big, big, big],
        out_specs=pl.BlockSpec((d, tt), lambda i, j: (0, i)),
        out_shape=jax.ShapeDtypeStruct((d, t), F32),
        scratch_shapes=[pltpu.VMEM((te, tt), BF16)],
        compiler_params=_params(("parallel", "arbitrary"), vmem_mb=56),
    )(h2d, u_bf, vt_bf, rk2, c1, e1, e2)


def _res_t_kernel(x_ref, yt_ref, g_ref, o_ref):
    o_ref[0] = x_ref[0] + g_ref[0] * yt_ref[...].T


def _residual_t(x, y_t, modr, row_gate, tm=512, tn=512):
    b, s, d = x.shape
    nt = s // tm
    return pl.pallas_call(
        _res_t_kernel,
        name="residual_t",
        grid=(b, nt, d // tn),
        in_specs=[pl.BlockSpec((1, tm, tn), lambda bi, i, j: (bi, i, j)),
                  pl.BlockSpec((tn, tm), lambda bi, i, j: (j, bi * nt + i)),
                  pl.BlockSpec((1, 1, tn), lambda bi, i, j: (row_gate(bi), 0, j))],
        out_specs=pl.BlockSpec((1, tm, tn), lambda bi, i, j: (bi, i, j)),
        out_shape=jax.ShapeDtypeStruct((b, s, d), F32),
        compiler_params=_params(("parallel", "parallel", "parallel")),
    )(x, y_t, modr)


def _pad_cols(a, width):
    return jnp.pad(a, ((0, 0), (0, width - a.shape[1])))


def _mla_weights(w_in, w_uq, g_qn, g_kn):
    w_in_p = _pad_cols(w_in, MLA_Q_RANK + MLA_KV_RANK + LANES).astype(BF16)
    w_uq_p = jnp.pad(w_uq.reshape(MLA_Q_RANK, MLA_HEADS, MLA_QK),
                     ((0, 0), (0, 0), (0, MLA_HEAD_PAD - MLA_QK)))
    w_uq_p = w_uq_p.reshape(MLA_Q_RANK, MLA_HEADS * MLA_HEAD_PAD).astype(BF16)
    pad = lambda g: _pad_cols(g.reshape(1, MLA_QK), MLA_HEAD_PAD)
    return w_in_p, w_uq_p, pad(g_qn), pad(g_kn)


def _peer(x, layer, norm_row, modr, row, w_q, sub_keys, u_tab, v_tab):
    b, s, d = x.shape
    q, hmod = _norm_matmul(x, norm_row, modr, row(layer, 4), row(layer, 3), w_q.astype(BF16), emit_h=True)
    rk2, c1, e1, e2 = _peer_topk(q.reshape(b * s, -1), sub_keys)
    y_t = _peer_dense(hmod.reshape(b * s, d), u_tab.astype(BF16), v_tab.T.astype(BF16), rk2, c1, e1, e2)
    return _residual_t(x, y_t, modr, row(layer, 5))


def kernel(x, c, positions, ada_w, ada_b, norm_g, mla_w_in, mla_g_q, mla_w_uq, mla_g_kv, mla_w_ukv, mla_g_qn, mla_g_kn, mla_w_o, dil_w_in, dil_g_qn, dil_g_kn, dil_w_o, peer_w_q, peer_sub_keys, peer_u, peer_v):
    b, s, d = x.shape
    depth = ada_w.shape[0]
    mod = _ada_mod(c, ada_w, ada_b)
    modr = mod.reshape(depth * b * 6, 1, d)
    row = lambda layer, part: (lambda bi: (layer * b + bi) * 6 + part)
    m_cos, m_sin_lo, m_sin_hi, d_cos, d_sin = _rope_tables(positions)

    for layer in range(depth):
        g1 = norm_g[layer, 0].reshape(1, d)
        g2 = norm_g[layer, 1].reshape(1, d)
        a = layer // 2
        if layer % 2 == 0:
            w_in_p, w_uq_p, g_qn_p, g_kn_p = _mla_weights(mla_w_in[a], mla_w_uq[a], mla_g_qn[a], mla_g_kn[a])
            (z,) = _norm_matmul(x, g1, modr, row(layer, 1), row(layer, 0), w_in_p, emit_h=False)
            q, k, v = _mla_prep(z, mla_g_q[a].reshape(1, -1), mla_g_kv[a].reshape(1, -1), w_uq_p,
                                mla_w_ukv[a].astype(BF16), g_qn_p, g_kn_p, (m_cos, m_sin_lo, m_sin_hi))
            o = _flash_attention(q, k, v)
            x = _proj_residual(o, mla_w_o[a].astype(BF16), x, modr, row(layer, 2))
        else:
            hmod = _norm_mod(x, g1, modr, row(layer, 1), row(layer, 0))
            w_in = dil_w_in[a].astype(BF16)
            ones = jnp.ones((len(DIL_GROUPS), DIL_HEAD_DIM), F32)
            gains = jnp.stack([dil_g_qn[a], dil_g_kn[a], ones], axis=1).reshape(-1, 1, DIL_HEAD_DIM)
            state = None
            for gi, (_, dil) in enumerate(DIL_GROUPS):
                qkv = _dil_project(hmod, w_in, gains, d_cos, d_sin, gi, dil)
                state = _dil_attention(qkv, state, is_last=gi == len(DIL_GROUPS) - 1)
            x = _proj_residual(state, dil_w_o[a].astype(BF16), x, modr, row(layer, 2))
        x = _peer(x, layer, g2, modr, row, peer_w_q[layer], peer_sub_keys[layer], peer_u[layer], peer_v[layer])
    return x
```

```python
import functools
import math

import jax
import jax.numpy as jnp
from jax import lax
from jax.experimental import pallas as pl
from jax.experimental.pallas import tpu as pltpu

F32 = jnp.float32
BF16 = jnp.bfloat16
EPS = 1e-6
ROPE_THETA = 10000.0
LOG2E = 1.4426950408889634
NEG_INF = float("-inf")
LANES = 128

MLA_HEADS = 16
MLA_Q_RANK = 512
MLA_KV_RANK = 512
MLA_NOPE = 128
MLA_ROPE = 64
MLA_QK = MLA_NOPE + MLA_ROPE
MLA_V = 128
MLA_HEAD_PAD = 256

DIL_GROUPS = ((128, 1), (512, 4), (2048, 16))
DIL_HEADS = 16
DIL_HEAD_DIM = 128
DIL_STEPS = 128

PEER_HEADS = 8
PEER_NKEYS = 128
PEER_TOPK = 16

_NT = (((1,), (1,)), ((), ()))


def _params(sem, vmem_mb=48):
    return pltpu.CompilerParams(dimension_semantics=sem, vmem_limit_bytes=vmem_mb << 20)


def _ada_kernel(c_ref, w_ref, b_ref, o_ref):
    c = c_ref[...]
    sc = c / (1.0 + jnp.exp(-c))
    o_ref[0] = jnp.dot(sc, w_ref[0], preferred_element_type=F32,
                       precision=lax.Precision.HIGHEST) + b_ref[0]


def _ada_mod(c, ada_w, ada_b, tn=768):
    depth, d, n = ada_w.shape
    b = c.shape[0]
    return pl.pallas_call(
        _ada_kernel,
        name="ada_mod",
        grid=(depth, n // tn),
        in_specs=[pl.BlockSpec((b, d), lambda l, j: (0, 0)),
                  pl.BlockSpec((1, d, tn), lambda l, j: (l, 0, j)),
                  pl.BlockSpec((1, 1, tn), lambda l, j: (l, 0, j))],
        out_specs=pl.BlockSpec((1, b, tn), lambda l, j: (l, 0, j)),
        out_shape=jax.ShapeDtypeStruct((depth, b, n), F32),
        compiler_params=_params(("parallel", "parallel")),
    )(c, ada_w, ada_b.reshape(depth, 1, n))


def _rope_tab_kernel(pos_ref, fm_ref, fd_ref, mc_ref, ms1_ref, ms2_ref, dc_ref, ds_ref):
    pos = pos_ref[0].astype(F32)
    lane = lax.broadcasted_iota(jnp.int32, (pos.shape[0], LANES), 1)
    am = pos * fm_ref[...]
    cm, sm = jnp.cos(am), jnp.sin(am)
    mc_ref[0] = jnp.where(lane < 2 * (MLA_ROPE // 2), cm, 0.0)
    ms1_ref[0] = jnp.where(lane < MLA_ROPE // 2, -sm, 0.0)
    ms2_ref[0] = jnp.where(lane < MLA_ROPE // 2, 0.0, jnp.where(lane < MLA_ROPE, sm, 0.0))
    ad = pos * fd_ref[...]
    dc_ref[0] = jnp.cos(ad)
    sd = jnp.sin(ad)
    ds_ref[0] = jnp.where(lane < DIL_HEAD_DIM // 2, -sd, sd)


def _rope_tables(positions, ts=512):
    b, s = positions.shape
    hm, hd = MLA_ROPE // 2, DIL_HEAD_DIM // 2
    inv_m = ROPE_THETA ** (-jnp.arange(hm, dtype=F32) / hm)
    inv_d = ROPE_THETA ** (-jnp.arange(hd, dtype=F32) / hd)
    fm = jnp.concatenate([inv_m, inv_m, jnp.zeros((LANES - 2 * hm,), F32)]).reshape(1, LANES)
    fd = jnp.concatenate([inv_d, inv_d]).reshape(1, LANES)
    tab = jax.ShapeDtypeStruct((b, s, LANES), F32)
    row = pl.BlockSpec((1, ts, LANES), lambda bi, i: (bi, i, 0))
    frq = pl.BlockSpec((1, LANES), lambda bi, i: (0, 0))
    return pl.pallas_call(
        _rope_tab_kernel,
        name="rope_tables",
        grid=(b, s // ts),
        in_specs=[pl.BlockSpec((1, ts, 1), lambda bi, i: (bi, i, 0)), frq, frq],
        out_specs=[row] * 5,
        out_shape=[tab] * 5,
        compiler_params=_params(("parallel", "parallel")),
    )(positions.reshape(b, s, 1), fm, fd)


def _norm_matmul_kernel(x_ref, g_ref, sc_ref, sh_ref, w_ref, o_ref, *h_out):
    x = x_ref[0]
    inv = lax.rsqrt(jnp.mean(x * x, axis=-1, keepdims=True) + EPS)
    h = (x * inv) * g_ref[...]
    h = (h * (1.0 + sc_ref[0]) + sh_ref[0]).astype(BF16)
    for h_ref in h_out:
        h_ref[0] = h
    o_ref[0] = jnp.dot(h, w_ref[...], preferred_element_type=F32)


def _norm_mod_kernel(x_ref, g_ref, sc_ref, sh_ref, o_ref):
    x = x_ref[0]
    inv = lax.rsqrt(jnp.mean(x * x, axis=-1, keepdims=True) + EPS)
    h = (x * inv) * g_ref[...]
    o_ref[0] = (h * (1.0 + sc_ref[0]) + sh_ref[0]).astype(o_ref.dtype)


def _norm_mod(x, g_row, modr, row_scale, row_shift, tm=512):
    b, s, d = x.shape
    return pl.pallas_call(
        _norm_mod_kernel,
        name="norm_mod",
        grid=(b, s // tm),
        in_specs=[pl.BlockSpec((1, tm, d), lambda bi, i: (bi, i, 0)),
                  pl.BlockSpec((1, d), lambda bi, i: (0, 0)),
                  pl.BlockSpec((1, 1, d), lambda bi, i: (row_scale(bi), 0, 0)),
                  pl.BlockSpec((1, 1, d), lambda bi, i: (row_shift(bi), 0, 0))],
        out_specs=pl.BlockSpec((1, tm, d), lambda bi, i: (bi, i, 0)),
        out_shape=jax.ShapeDtypeStruct((b, s, d), BF16),
        compiler_params=_params(("parallel", "parallel")),
    )(x, g_row, modr, modr)


def _norm_matmul(x, g_row, modr, row_scale, row_shift, w, emit_h, tm=512):
    b, s, d = x.shape
    n = w.shape[1]
    rows = lambda width: pl.BlockSpec((1, tm, width), lambda bi, i: (bi, i, 0))
    out_specs, out_shape = [rows(n)], [jax.ShapeDtypeStruct((b, s, n), F32)]
    if emit_h:
        out_specs.append(rows(d))
        out_shape.append(jax.ShapeDtypeStruct((b, s, d), BF16))
    return pl.pallas_call(
        _norm_matmul_kernel,
        name="norm_matmul",
        grid=(b, s // tm),
        in_specs=[rows(d),
                  pl.BlockSpec((1, d), lambda bi, i: (0, 0)),
                  pl.BlockSpec((1, 1, d), lambda bi, i: (row_scale(bi), 0, 0)),
                  pl.BlockSpec((1, 1, d), lambda bi, i: (row_shift(bi), 0, 0)),
                  pl.BlockSpec((d, n), lambda bi, i: (0, 0))],
        out_specs=out_specs,
        out_shape=out_shape,
        compiler_params=_params(("parallel", "parallel")),
    )(x, g_row, modr, modr, w)


def _proj_res_kernel(a_ref, w_ref, x_ref, g_ref, o_ref):
    y = jnp.dot(a_ref[0], w_ref[...], preferred_element_type=F32)
    o_ref[0] = x_ref[0] + g_ref[0] * y


def _proj_residual(a, w, x, modr, row_gate, tm=512):
    b, s, k = a.shape
    d = w.shape[1]
    tn = d
    return pl.pallas_call(
        _proj_res_kernel,
        name="proj_residual",
        grid=(b, s // tm, d // tn),
        in_specs=[pl.BlockSpec((1, tm, k), lambda bi, i, j: (bi, i, 0)),
                  pl.BlockSpec((k, tn), lambda bi, i, j: (0, j)),
                  pl.BlockSpec((1, tm, tn), lambda bi, i, j: (bi, i, j)),
                  pl.BlockSpec((1, 1, tn), lambda bi, i, j: (row_gate(bi), 0, j))],
        out_specs=pl.BlockSpec((1, tm, tn), lambda bi, i, j: (bi, i, j)),
        out_shape=jax.ShapeDtypeStruct((b, s, d), F32),
        compiler_params=_params(("parallel", "parallel", "parallel")),
    )(a, w, x, modr)


def _mla_prep_kernel(z_ref, gq_ref, gkv_ref, wuq_ref, wukv_ref, gqn_ref, gkn_ref,
                     c_ref, s1_ref, s2_ref, q_ref, k_ref, vt_ref, *, q_scale):
    z = z_ref[0]
    cq = z[:, :MLA_Q_RANK]
    ckv = z[:, MLA_Q_RANK:MLA_Q_RANK + MLA_KV_RANK]
    kr = z[:, MLA_Q_RANK + MLA_KV_RANK:]

    def rms(t, g):
        return (t * lax.rsqrt(jnp.mean(t * t, axis=-1, keepdims=True) + EPS)) * g

    qa = jnp.dot(rms(cq, gq_ref[...]).astype(BF16), wuq_ref[...], preferred_element_type=F32)
    kva = jnp.dot(rms(ckv, gkv_ref[...]).astype(BF16), wukv_ref[...], preferred_element_type=F32)
    cos, sin_lo, sin_hi = c_ref[0], s1_ref[0], s2_ref[0]

    def rope(t):
        return (t * cos + pltpu.roll(t, LANES - MLA_ROPE // 2, 1) * sin_lo
                + pltpu.roll(t, MLA_ROPE // 2, 1) * sin_hi)

    gqn, gkn = gqn_ref[...], gkn_ref[...]
    kr_ss = jnp.sum(kr * kr, axis=-1, keepdims=True)
    kr_rot = rope(kr * gkn[:, MLA_NOPE:])
    for h in range(MLA_HEADS):
        lo = h * MLA_HEAD_PAD
        qh = qa[:, lo:lo + MLA_HEAD_PAD]
        inv = lax.rsqrt(jnp.sum(qh * qh, axis=-1, keepdims=True) * (1.0 / MLA_QK) + EPS) * q_scale
        q_ref[0, h, :, :MLA_NOPE] = ((qh[:, :MLA_NOPE] * gqn[:, :MLA_NOPE]) * inv).astype(BF16)
        q_ref[0, h, :, MLA_NOPE:] = (rope(qh[:, MLA_NOPE:] * gqn[:, MLA_NOPE:]) * inv).astype(BF16)
        kn = kva[:, lo:lo + MLA_NOPE]
        kss = jnp.sum(kn * kn, axis=-1, keepdims=True) + kr_ss
        kinv = lax.rsqrt(kss * (1.0 / MLA_QK) + EPS)
        k_ref[0, h, :, :MLA_NOPE] = ((kn * gkn[:, :MLA_NOPE]) * kinv).astype(BF16)
        k_ref[0, h, :, MLA_NOPE:] = (kr_rot * kinv).astype(BF16)
        vt_ref[0, h] = kva[:, lo + MLA_NOPE:lo + MLA_HEAD_PAD].T.astype(BF16)


def _mla_prep(z, g_q, g_kv, w_uq_p, w_ukv, g_qn_p, g_kn_p, tabs, tm=256):
    b, s, zw = z.shape
    nh = MLA_HEADS
    row = lambda w: pl.BlockSpec((1, w), lambda bi, i: (0, 0))
    full = lambda a: pl.BlockSpec(a.shape, lambda bi, i: (0, 0))
    tab = pl.BlockSpec((1, tm, LANES), lambda bi, i: (bi, i, 0))
    head = lambda w: pl.BlockSpec((1, nh, tm, w), lambda bi, i: (bi, 0, i, 0))
    kern = functools.partial(_mla_prep_kernel, q_scale=MLA_QK ** -0.5 * LOG2E)
    return pl.pallas_call(
        kern,
        name="mla_prep",
        grid=(b, s // tm),
        in_specs=[pl.BlockSpec((1, tm, zw), lambda bi, i: (bi, i, 0)),
                  row(MLA_Q_RANK), row(MLA_KV_RANK), full(w_uq_p), full(w_ukv),
                  row(MLA_HEAD_PAD), row(MLA_HEAD_PAD), tab, tab, tab],
        out_specs=[head(MLA_HEAD_PAD), head(MLA_HEAD_PAD),
                   pl.BlockSpec((1, nh, MLA_V, tm), lambda bi, i: (bi, 0, 0, i))],
        out_shape=[jax.ShapeDtypeStruct((b, nh, s, MLA_HEAD_PAD), BF16),
                   jax.ShapeDtypeStruct((b, nh, s, MLA_HEAD_PAD), BF16),
                   jax.ShapeDtypeStruct((b, nh, MLA_V, s), BF16)],
        compiler_params=_params(("parallel", "parallel")),
    )(z, g_q, g_kv, w_uq_p, w_ukv, g_qn_p, g_kn_p, *tabs)


def _flash_kernel(q_ref, k_ref, vt_ref, o_ref, acc_ref, *, t, nsub):
    qi = pl.program_id(2)
    acc_ref[...] = jnp.zeros(acc_ref.shape, F32)
    qs = [q_ref[0, 0, c * t:(c + 1) * t, :] for c in range(nsub)]

    def scores(c, j, diagonal):
        start = pl.multiple_of(j * t, t)
        k = k_ref[0, 0, pl.ds(start, t), :]
        st = lax.dot_general(k, qs[c], _NT, preferred_element_type=F32)
        if diagonal:
            key = lax.broadcasted_iota(jnp.int32, (t, t), 0)
            qry = lax.broadcasted_iota(jnp.int32, (t, t), 1)
            st = jnp.where(key <= qry, st, NEG_INF)
        return st

    def soft(st, m, l):
        m_new = jnp.maximum(m, jnp.max(st, axis=0, keepdims=True))
        p = jnp.exp2(st - m_new)
        alpha = jnp.exp2(m - m_new)
        l_new = alpha * l + jnp.sum(p, axis=0, keepdims=True)
        return p.astype(BF16), alpha, m_new, l_new

    def accum(c, j, p, alpha):
        start = pl.multiple_of(j * t, t)
        vt = vt_ref[0, 0, :, pl.ds(start, t)]
        acc_ref[c] = alpha * acc_ref[c] + jnp.dot(vt, p, preferred_element_type=F32)

    def multi(chains, j, carry, diag_chain):
        carry = list(carry)
        sts = [scores(c, j, c == diag_chain) for c in chains]
        ps = []
        for c, st in zip(chains, sts):
            p, alpha, carry[2 * c], carry[2 * c + 1] = soft(st, carry[2 * c], carry[2 * c + 1])
            ps.append((p, alpha))
        for c, (p, alpha) in zip(chains, ps):
            accum(c, j, p, alpha)
        return tuple(carry)

    def body(j, carry):
        return multi(range(nsub), j, carry, -1)

    init = []
    for c in range(nsub):
        init += [jnp.full((1, t), NEG_INF, F32), jnp.zeros((1, t), F32)]
    carry = lax.fori_loop(0, nsub * qi, body, tuple(init))
    for d in range(nsub):
        carry = multi(range(d, nsub), nsub * qi + d, carry, d)
    for c in range(nsub):
        ot = acc_ref[c] / carry[2 * c + 1]
        o_ref[0, c * t:(c + 1) * t, :] = ot.T.astype(o_ref.dtype)


def _flash_attention(q, k, vt, t=512, nsub=8):
    b, nh, s, dq = q.shape
    dv = vt.shape[2]
    tq = t * nsub
    return pl.pallas_call(
        functools.partial(_flash_kernel, t=t, nsub=nsub),
        name="mla_flash",
        grid=(b, nh, s // tq),
        in_specs=[pl.BlockSpec((1, 1, tq, dq), lambda bi, h, i: (bi, h, i, 0)),
                  pl.BlockSpec((1, 1, s, dq), lambda bi, h, i: (bi, h, 0, 0)),
                  pl.BlockSpec((1, 1, dv, s), lambda bi, h, i: (bi, h, 0, 0))],
        out_specs=pl.BlockSpec((1, tq, dv), lambda bi, h, i: (bi, i, h)),
        out_shape=jax.ShapeDtypeStruct((b, s, nh * dv), BF16),
        scratch_shapes=[pltpu.VMEM((nsub, dv, t), F32)],
        compiler_params=_params(("parallel", "parallel", "arbitrary")),
    )(q, k, vt)


_DIL_TN = 1024
_DIL_SUB = 256
_DIL_HPT = _DIL_TN // DIL_HEAD_DIM
_DIL_TPP = DIL_HEADS // _DIL_HPT


def _dil_proj_kernel(h_ref, w_ref, gain_ref, c_ref, s_ref, o_ref, *, q_scale):
    part = pl.program_id(3) // _DIL_TPP
    nsub = _DIL_TN // _DIL_SUB
    hps = _DIL_SUB // LANES
    h = h_ref[0]
    is_v = part == 2
    cos, sin = c_ref[0], s_ref[0]
    gain = gain_ref[0]
    post = jnp.where(part == 0, q_scale, 1.0)
    ri = lax.broadcasted_iota(jnp.int32, (_DIL_SUB, _DIL_SUB), 0) // LANES
    ci = lax.broadcasted_iota(jnp.int32, (_DIL_SUB, _DIL_SUB), 1) // LANES
    seg = jnp.where(ri == ci, 1.0, 0.0).astype(BF16)

    def main(sb):
        return jnp.dot(h, w_ref[:, sb * _DIL_SUB:(sb + 1) * _DIL_SUB], preferred_element_type=F32)

    def sumsq(y):
        return jnp.dot((y * y).astype(BF16), seg, preferred_element_type=F32)

    def finish(sb, y, ss):
        inv = lax.rsqrt(ss * (1.0 / DIL_HEAD_DIM) + EPS)
        for hh in range(hps):
            cols = slice(hh * LANES, (hh + 1) * LANES)
            yh = y[:, cols]
            yn = (yh * inv[:, cols]) * gain
            rot = yn * cos + pltpu.roll(yn, DIL_HEAD_DIM // 2, 1) * sin
            o_ref[0, 0, 0, sb * hps + hh] = jnp.where(is_v, yh, rot * post).astype(BF16)

    pairs = [(sb, sb + 1) for sb in range(0, nsub, 2)]
    ys = {sb: main(sb) for sb in pairs[0]}
    for n, pair in enumerate(pairs):
        ss = {sb: sumsq(ys[sb]) for sb in pair}
        if n + 1 < len(pairs):
            ys.update({sb: main(sb) for sb in pairs[n + 1]})
        for sb in pair:
            finish(sb, ys[sb], ss[sb])


def _dil_project(h, w_in, gains, cos_t, sin_t, group, dil):
    b, s, d = h.shape
    l = s // dil
    tl = min(1024, l)
    tiles = 3 * _DIL_TPP
    kern = functools.partial(_dil_proj_kernel, q_scale=DIL_HEAD_DIM ** -0.5 * LOG2E)
    return pl.pallas_call(
        kern,
        name=f"dil_proj_d{dil}",
        grid=(b, dil, l // tl, tiles),
        in_specs=[pl.BlockSpec((1, tl, d), lambda bi, r, i, j: (bi, i, r)),
                  pl.BlockSpec((d, _DIL_TN), lambda bi, r, i, j: (0, group * tiles + j)),
                  pl.BlockSpec((1, 1, LANES), lambda bi, r, i, j: (group * 3 + j // _DIL_TPP, 0, 0)),
                  pl.BlockSpec((1, tl, LANES), lambda bi, r, i, j: (bi, i, r)),
                  pl.BlockSpec((1, tl, LANES), lambda bi, r, i, j: (bi, i, r))],
        out_specs=pl.BlockSpec((1, 1, 1, _DIL_HPT, tl, LANES),
                               lambda bi, r, i, j: (bi, r, j // _DIL_TPP, j % _DIL_TPP, i, 0)),
        out_shape=jax.ShapeDtypeStruct((b, dil, 3, DIL_HEADS, l, LANES), BF16),
        compiler_params=_params(("parallel", "parallel", "parallel", "arbitrary")),
    )(h.reshape(b, l, dil * d), w_in, gains,
      cos_t.reshape(b, l, dil * LANES), sin_t.reshape(b, l, dil * LANES))


def _dil_attn_kernel(*refs, tq, has_state, is_last):
    q_ref, kc_ref, kp_ref, vc_ref, vp_ref = refs[:5]
    refs = refs[5:]
    if has_state:
        acc_in, m_in, l_in = refs[:3]
        refs = refs[3:]
    if is_last:
        (o_ref,) = refs
    else:
        acc_out, m_out, l_out = refs

    first = pl.program_id(2) == 0
    row = lax.broadcasted_iota(jnp.int32, (tq, tq), 0)
    col = lax.broadcasted_iota(jnp.int32, (tq, tq), 1)
    dist = row - col
    bias_c = jnp.where(dist >= 0, jnp.where(dist <= DIL_STEPS, 0.0, NEG_INF), NEG_INF)
    rowp = lax.broadcasted_iota(jnp.int32, (tq, DIL_STEPS), 0)
    colp = lax.broadcasted_iota(jnp.int32, (tq, DIL_STEPS), 1)
    bias_p = jnp.where(colp >= rowp, jnp.where(first, NEG_INF, 0.0), NEG_INF)
    lane = lax.broadcasted_iota(jnp.int32, (tq, LANES), 1)
    if has_state:
        m_old, l_old = m_in[0], l_in[0]
    m_tile = jnp.zeros((tq, LANES), F32)
    l_tile = jnp.zeros((tq, LANES), F32)

    for h in range(DIL_HEADS):
        q = q_ref[0, 0, 0, h]
        sc = lax.dot_general(q, kc_ref[0, 0, 0, h], _NT, preferred_element_type=F32) + bias_c
        sp = lax.dot_general(q, kp_ref[0, 0, 0, h], _NT, preferred_element_type=F32) + bias_p
        m_new = jnp.maximum(jnp.max(sc, axis=-1, keepdims=True), jnp.max(sp, axis=-1, keepdims=True))
        if has_state:
            m_prev = m_old[:, h:h + 1]
            m_new = jnp.maximum(m_new, m_prev)
            alpha = jnp.exp2(m_prev - m_new)
        pc = jnp.exp2(sc - m_new)
        pp = jnp.exp2(sp - m_new)
        l_new = jnp.sum(pc, axis=-1, keepdims=True) + jnp.sum(pp, axis=-1, keepdims=True)
        acc = (jnp.dot(pc.astype(BF16), vc_ref[0, 0, 0, h], preferred_element_type=F32)
               + jnp.dot(pp.astype(BF16), vp_ref[0, 0, 0, h], preferred_element_type=F32))
        cols = slice(h * LANES, (h + 1) * LANES)
        if has_state:
            l_new = l_new + alpha * l_old[:, h:h + 1]
            acc = acc + alpha * acc_in[0, :, cols]
        if is_last:
            o_ref[0, :, cols] = (acc / l_new).astype(o_ref.dtype)
        else:
            acc_out[0, :, cols] = acc
            m_tile = jnp.where(lane == h, m_new, m_tile)
            l_tile = jnp.where(lane == h, l_new, l_tile)
    if not is_last:
        m_out[0] = m_tile
        l_out[0] = l_tile


def _dil_attention(qkv, state, is_last, tq=512):
    b, dil, _, nh, l, dh = qkv.shape
    tq = min(tq, l)
    width = nh * dh
    ratio = tq // DIL_STEPS
    cur = lambda part: pl.BlockSpec((1, 1, 1, nh, tq, dh), lambda bi, r, i: (bi, r, part, 0, i, 0))
    prev = lambda part: pl.BlockSpec(
        (1, 1, 1, nh, DIL_STEPS, dh),
        lambda bi, r, i: (bi, r, part, 0, jnp.maximum(i * ratio - 1, 0), 0))
    wide = pl.BlockSpec((1, tq, width), lambda bi, r, i: (bi, i, r))
    thin = pl.BlockSpec((1, tq, LANES), lambda bi, r, i: (bi, i, r))
    in_specs = [cur(0), cur(1), prev(1), cur(2), prev(2)]
    args = [qkv] * 5
    if state is not None:
        in_specs += [wide, thin, thin]
        args += [state[0].reshape(b, l, dil * width), state[1].reshape(b, l, dil * LANES),
                 state[2].reshape(b, l, dil * LANES)]
    if is_last:
        out_specs = wide
        out_shape = jax.ShapeDtypeStruct((b, l, dil * width), BF16)
    else:
        out_specs = [wide, thin, thin]
        out_shape = [jax.ShapeDtypeStruct((b, l, dil * width), F32),
                     jax.ShapeDtypeStruct((b, l, dil * LANES), F32),
                     jax.ShapeDtypeStruct((b, l, dil * LANES), F32)]
    kern = functools.partial(_dil_attn_kernel, tq=tq, has_state=state is not None, is_last=is_last)
    out = pl.pallas_call(
        kern,
        name=f"dil_attn_d{dil}",
        grid=(b, dil, l // tq),
        in_specs=in_specs,
        out_specs=out_specs,
        out_shape=out_shape,
        compiler_params=_params(("parallel", "parallel", "parallel")),
    )(*args)
    s = l * dil
    if is_last:
        return out.reshape(b, s, width)
    return (out[0].reshape(b, s, width), out[1].reshape(b, s, LANES), out[2].reshape(b, s, LANES))


def _batcher_pairs(n):
    pairs = []
    p = 1
    while p < n:
        k = p
        while k >= 1:
            for j in range(k % p, n - k, 2 * k):
                for i in range(min(k, n - j - k)):
                    if (i + j) // (2 * p) == (i + j + k) // (2 * p):
                        pairs.append((i + j, i + j + k))
            k //= 2
        p *= 2
    return pairs


def _sorted_top(vs, count):
    vs = list(vs) + [None] * (count - len(vs))

    def exchange(i, j):
        a, b = vs[i], vs[j]
        if b is None:
            return
        if a is None:
            vs[i], vs[j] = b, None
            return
        vs[i], vs[j] = jnp.maximum(a, b), jnp.minimum(a, b)

    for i, j in _batcher_pairs(count):
        exchange(i, j)
    shape = next(v.shape for v in vs if v is not None)
    vs = [jnp.full(shape, NEG_INF, F32) if v is None else v for v in vs]
    for shift in (4, 2, 1):
        other = [pltpu.roll(v, shift, 0) for v in vs]
        vs = [jnp.maximum(vs[i], other[count - 1 - i]) for i in range(count)]
        k = count // 2
        while k >= 1:
            for i in range(count):
                if i & k == 0:
                    exchange(i, i + k)
            k //= 2
    return vs


def _peer_topk_kernel(q_ref, keys_ref, rk2_ref, c1_ref, e1_ref, e2_ref):
    k = PEER_TOPK
    tt = q_ref.shape[0]
    nslab = PEER_NKEYS // 8
    sub = lax.broadcasted_iota(jnp.int32, (8, tt), 0)

    def by_sublane(vals):
        out = vals[0]
        for i in range(1, 8):
            out = jnp.where(sub == i, vals[i], out)
        return out

    def count(slab, thr):
        return jnp.sum(jnp.where(slab >= thr, 1.0, 0.0), axis=0, keepdims=True)

    for h in range(PEER_HEADS):
        sts, tops = [], []
        for p in range(2):
            hp = 2 * h + p
            qs = q_ref[:, hp * PEER_NKEYS:(hp + 1) * PEER_NKEYS]
            st = lax.dot_general(keys_ref[hp], qs, _NT, preferred_element_type=F32,
                                 precision=lax.Precision.HIGHEST)
            slabs = [st[8 * i:8 * i + 8, :] for i in range(nslab)]
            sts.append(slabs)
            tops.append(_sorted_top(slabs, k))
        a, bv = tops
        b_lo, b_hi, a_hi = by_sublane(bv[:8]), by_sublane(bv[8:]), by_sublane(a[8:])
        cand = [a[0] + b_lo, a[0] + b_hi, a[1] + b_lo]
        for i in range(2, 8):
            cand.append(jnp.where(sub < k // (i + 1), a[i] + b_lo, NEG_INF))
        cand.append(a_hi + bv[0])
        best = _sorted_top(cand, k)
        top, thr = best[0], best[k - 1]
        z = jnp.exp(best[0] - top)
        for r in range(1, k):
            z = z + jnp.exp(best[r] - top)
        cnt = [count(cand[0], thr) + count(cand[1], thr)]
        cnt += [count(cand[i + 1], thr) for i in range(1, 8)]
        tail = jnp.where(cand[9] >= thr, 1.0, 0.0)
        cnt += [tail[i:i + 1] for i in range(8)]
        inv_z = 1.0 / z
        for i in range(nslab):
            s1, s2 = sts[0][i], sts[1][i]
            c1 = jnp.zeros(s1.shape, F32)
            rk = jnp.full(s2.shape, float(k), F32)
            for r in range(k - 1, -1, -1):
                c1 = jnp.where(s1 >= a[r], cnt[r], c1)
                rk = jnp.where(s2 >= bv[r], float(r), rk)
            rows = slice(8 * i, 8 * i + 8)
            c1_ref[h, rows, :] = c1
            rk2_ref[h, rows, :] = rk.astype(BF16)
            e1_ref[h, rows, :] = jnp.exp(s1 - a[0]) * inv_z
            e2_ref[h, rows, :] = jnp.exp(s2 - bv[0]).astype(BF16)


def _peer_topk(q2d, sub_keys, tt=512):
    t = q2d.shape[0]
    nh = PEER_HEADS
    keys = sub_keys.reshape(2 * nh, PEER_NKEYS, sub_keys.shape[-1])
    big = pl.BlockSpec((nh, PEER_NKEYS, tt), lambda i: (0, 0, i))
    return pl.pallas_call(
        _peer_topk_kernel,
        name="peer_topk",
        grid=(t // tt,),
        in_specs=[pl.BlockSpec((tt, q2d.shape[1]), lambda i: (i, 0)),
                  pl.BlockSpec(keys.shape, lambda i: (0, 0, 0))],
        out_specs=[big, big, big, big],
        out_shape=[jax.ShapeDtypeStruct((nh, PEER_NKEYS, t), BF16),
                   jax.ShapeDtypeStruct((nh, PEER_NKEYS, t), F32),
                   jax.ShapeDtypeStruct((nh, PEER_NKEYS, t), F32),
                   jax.ShapeDtypeStruct((nh, PEER_NKEYS, t), BF16)],
        compiler_params=_params(("parallel",)),
    )(q2d, keys)


def _peer_dense_kernel(x_ref, u_ref, vt_ref, rk2_ref, c1_ref, e1_ref, e2_ref, o_ref, act_ref, *, te):
    j = pl.program_id(1)

    @pl.when(j == 0)
    def _():
        o_ref[...] = jnp.zeros(o_ref.shape, F32)

    half = te // 2
    per = half // PEER_NKEYS
    pre = [lax.dot_general(u_ref[sb * half:(sb + 1) * half, :], x_ref[...], _NT,
                           preferred_element_type=F32) for sb in range(2)]
    for sb in range(2):
        a = pre[sb]
        ge = (0.5 * a * (1.0 + lax.erf(a * (1.0 / math.sqrt(2.0))))).astype(BF16)
        for ii in range(per):
            i1 = j * (2 * per) + sb * per + ii
            gate = jnp.zeros((PEER_NKEYS, x_ref.shape[0]), BF16)
            for h in range(PEER_HEADS):
                c1 = c1_ref[h, pl.ds(i1, 1), :].astype(BF16)
                e1 = e1_ref[h, pl.ds(i1, 1), :].astype(BF16)
                gate = gate + jnp.where(rk2_ref[h] < c1, e2_ref[h], jnp.zeros((), BF16)) * e1
            lo = sb * half + ii * PEER_NKEYS
            act_ref[lo:lo + PEER_NKEYS, :] = ge[ii * PEER_NKEYS:(ii + 1) * PEER_NKEYS, :] * gate
    o_ref[...] += jnp.dot(vt_ref[...], act_ref[...], preferred_element_type=F32)


def _peer_dense(h2d, u_bf, vt_bf, rk2, c1, e1, e2, tt=512, te=1024):
    t, d = h2d.shape
    ne = u_bf.shape[0]
    big = pl.BlockSpec((PEER_HEADS, PEER_NKEYS, tt), lambda i, j: (0, 0, i))
    return pl.pallas_call(
        functools.partial(_peer_dense_kernel, te=te),
        name="peer_dense",
        grid=(t // tt, ne // te),
        in_specs=[pl.BlockSpec((tt, d), lambda i, j: (i, 0)),
                  pl.BlockSpec((te, d), lambda i, j: (j, 0)),
                  pl.BlockSpec((d, te), lambda i, j: (0, j)),
                  big, big, big, big],
        out_specs=pl.BlockSpec((d, tt), lambda i, j: (0, i)),
        out_shape=jax.ShapeDtypeStruct((d, t), F32),
        scratch_shapes=[pltpu.VMEM((te, tt), BF16)],
        compiler_params=_params(("parallel", "arbitrary"), vmem_mb=56),
    )(h2d, u_bf, vt_bf, rk2, c1, e1, e2)


def _res_t_kernel(x_ref, yt_ref, g_ref, o_ref):
    o_ref[0] = x_ref[0] + g_ref[0] * yt_ref[...].T


def _residual_t(x, y_t, modr, row_gate, tm=512, tn=512):
    b, s, d = x.shape
    nt = s // tm
    return pl.pallas_call(
        _res_t_kernel,
        name="residual_t",
        grid=(b, nt, d // tn),
        in_specs=[pl.BlockSpec((1, tm, tn), lambda bi, i, j: (bi, i, j)),
                  pl.BlockSpec((tn, tm), lambda bi, i, j: (j, bi * nt + i)),
                  pl.BlockSpec((1, 1, tn), lambda bi, i, j: (row_gate(bi), 0, j))],
        out_specs=pl.BlockSpec((1, tm, tn), lambda bi, i, j: (bi, i, j)),
        out_shape=jax.ShapeDtypeStruct((b, s, d), F32),
        compiler_params=_params(("parallel", "parallel", "parallel")),
    )(x, y_t, modr)


def _pad_cols(a, width):
    return jnp.pad(a, ((0, 0), (0, width - a.shape[1])))


def _mla_weights(w_in, w_uq, g_qn, g_kn):
    w_in_p = _pad_cols(w_in, MLA_Q_RANK + MLA_KV_RANK + LANES).astype(BF16)
    w_uq_p = jnp.pad(w_uq.reshape(MLA_Q_RANK, MLA_HEADS, MLA_QK),
                     ((0, 0), (0, 0), (0, MLA_HEAD_PAD - MLA_QK)))
    w_uq_p = w_uq_p.reshape(MLA_Q_RANK, MLA_HEADS * MLA_HEAD_PAD).astype(BF16)
    pad = lambda g: _pad_cols(g.reshape(1, MLA_QK), MLA_HEAD_PAD)
    return w_in_p, w_uq_p, pad(g_qn), pad(g_kn)


def _peer(x, layer, norm_row, modr, row, w_q, sub_keys, u_tab, v_tab):
    b, s, d = x.shape
    q, hmod = _norm_matmul(x, norm_row, modr, row(layer, 4), row(layer, 3), w_q.astype(BF16), emit_h=True)
    rk2, c1, e1, e2 = _peer_topk(q.reshape(b * s, -1), sub_keys)
    y_t = _peer_dense(hmod.reshape(b * s, d), u_tab.astype(BF16), v_tab.T.astype(BF16), rk2, c1, e1, e2)
    return _residual_t(x, y_t, modr, row(layer, 5))


def kernel(x, c, positions, ada_w, ada_b, norm_g, mla_w_in, mla_g_q, mla_w_uq, mla_g_kv, mla_w_ukv, mla_g_qn, mla_g_kn, mla_w_o, dil_w_in, dil_g_qn, dil_g_kn, dil_w_o, peer_w_q, peer_sub_keys, peer_u, peer_v):
    b, s, d = x.shape
    depth = ada_w.shape[0]
    mod = _ada_mod(c, ada_w, ada_b)
    modr = mod.reshape(depth * b * 6, 1, d)
    row = lambda layer, part: (lambda bi: (layer * b + bi) * 6 + part)
    m_cos, m_sin_lo, m_sin_hi, d_cos, d_sin = _rope_tables(positions)

    for layer in range(depth):
        g1 = norm_g[layer, 0].reshape(1, d)
        g2 = norm_g[layer, 1].reshape(1, d)
        a = layer // 2
        if layer % 2 == 0:
            w_in_p, w_uq_p, g_qn_p, g_kn_p = _mla_weights(mla_w_in[a], mla_w_uq[a], mla_g_qn[a], mla_g_kn[a])
            (z,) = _norm_matmul(x, g1, modr, row(layer, 1), row(layer, 0), w_in_p, emit_h=False)
            q, k, v = _mla_prep(z, mla_g_q[a].reshape(1, -1), mla_g_kv[a].reshape(1, -1), w_uq_p,
                                mla_w_ukv[a].astype(BF16), g_qn_p, g_kn_p, (m_cos, m_sin_lo, m_sin_hi))
            o = _flash_attention(q, k, v)
            x = _proj_residual(o, mla_w_o[a].astype(BF16), x, modr, row(layer, 2))
        else:
            hmod = _norm_mod(x, g1, modr, row(layer, 1), row(layer, 0))
            w_in = dil_w_in[a].astype(BF16)
            ones = jnp.ones((len(DIL_GROUPS), DIL_HEAD_DIM), F32)
            gains = jnp.stack([dil_g_qn[a], dil_g_kn[a], ones], axis=1).reshape(-1, 1, DIL_HEAD_DIM)
            state = None
            for gi in reversed(range(len(DIL_GROUPS))):
                dil = DIL_GROUPS[gi][1]
                qkv = _dil_project(hmod, w_in, gains, d_cos, d_sin, gi, dil)
                state = _dil_attention(qkv, state, is_last=gi == 0)
            x = _proj_residual(state, dil_w_o[a].astype(BF16), x, modr, row(layer, 2))
        x = _peer(x, layer, g2, modr, row, peer_w_q[layer], peer_sub_keys[layer], peer_u[layer], peer_v[layer])
    return x
```

```python
import functools
import math

import jax
import jax.numpy as jnp
from jax import lax
from jax.experimental import pallas as pl
from jax.experimental.pallas import tpu as pltpu

F32 = jnp.float32
BF16 = jnp.bfloat16
EPS = 1e-6
ROPE_THETA = 10000.0
LOG2E = 1.4426950408889634
NEG_INF = float("-inf")
LANES = 128

MLA_HEADS = 16
MLA_Q_RANK = 512
MLA_KV_RANK = 512
MLA_NOPE = 128
MLA_ROPE = 64
MLA_QK = MLA_NOPE + MLA_ROPE
MLA_V = 128
MLA_HEAD_PAD = 256

DIL_GROUPS = ((128, 1), (512, 4), (2048, 16))
DIL_HEADS = 16
DIL_HEAD_DIM = 128
DIL_STEPS = 128

PEER_HEADS = 8
PEER_NKEYS = 128
PEER_TOPK = 16

_NT = (((1,), (1,)), ((), ()))


def _params(sem, vmem_mb=48):
    return pltpu.CompilerParams(dimension_semantics=sem, vmem_limit_bytes=vmem_mb << 20)


def _ada_kernel(c_ref, w_ref, b_ref, o_ref):
    c = c_ref[...]
    sc = c / (1.0 + jnp.exp(-c))
    o_ref[0] = jnp.dot(sc, w_ref[0], preferred_element_type=F32,
                       precision=lax.Precision.HIGHEST) + b_ref[0]


def _ada_mod(c, ada_w, ada_b, tn=768):
    depth, d, n = ada_w.shape
    b = c.shape[0]
    return pl.pallas_call(
        _ada_kernel,
        name="ada_mod",
        grid=(depth, n // tn),
        in_specs=[pl.BlockSpec((b, d), lambda l, j: (0, 0)),
                  pl.BlockSpec((1, d, tn), lambda l, j: (l, 0, j)),
                  pl.BlockSpec((1, 1, tn), lambda l, j: (l, 0, j))],
        out_specs=pl.BlockSpec((1, b, tn), lambda l, j: (l, 0, j)),
        out_shape=jax.ShapeDtypeStruct((depth, b, n), F32),
        compiler_params=_params(("parallel", "parallel")),
    )(c, ada_w, ada_b.reshape(depth, 1, n))


def _rope_tab_kernel(pos_ref, fm_ref, fd_ref, mc_ref, ms1_ref, ms2_ref, dc_ref, ds_ref):
    pos = pos_ref[0].astype(F32)
    lane = lax.broadcasted_iota(jnp.int32, (pos.shape[0], LANES), 1)
    am = pos * fm_ref[...]
    cm, sm = jnp.cos(am), jnp.sin(am)
    mc_ref[0] = jnp.where(lane < 2 * (MLA_ROPE // 2), cm, 0.0)
    ms1_ref[0] = jnp.where(lane < MLA_ROPE // 2, -sm, 0.0)
    ms2_ref[0] = jnp.where(lane < MLA_ROPE // 2, 0.0, jnp.where(lane < MLA_ROPE, sm, 0.0))
    ad = pos * fd_ref[...]
    dc_ref[0] = jnp.cos(ad)
    sd = jnp.sin(ad)
    ds_ref[0] = jnp.where(lane < DIL_HEAD_DIM // 2, -sd, sd)


def _rope_tables(positions, ts=512):
    b, s = positions.shape
    hm, hd = MLA_ROPE // 2, DIL_HEAD_DIM // 2
    inv_m = ROPE_THETA ** (-jnp.arange(hm, dtype=F32) / hm)
    inv_d = ROPE_THETA ** (-jnp.arange(hd, dtype=F32) / hd)
    fm = jnp.concatenate([inv_m, inv_m, jnp.zeros((LANES - 2 * hm,), F32)]).reshape(1, LANES)
    fd = jnp.concatenate([inv_d, inv_d]).reshape(1, LANES)
    tab = jax.ShapeDtypeStruct((b, s, LANES), F32)
    row = pl.BlockSpec((1, ts, LANES), lambda bi, i: (bi, i, 0))
    frq = pl.BlockSpec((1, LANES), lambda bi, i: (0, 0))
    return pl.pallas_call(
        _rope_tab_kernel,
        name="rope_tables",
        grid=(b, s // ts),
        in_specs=[pl.BlockSpec((1, ts, 1), lambda bi, i: (bi, i, 0)), frq, frq],
        out_specs=[row] * 5,
        out_shape=[tab] * 5,
        compiler_params=_params(("parallel", "parallel")),
    )(positions.reshape(b, s, 1), fm, fd)


def _norm_matmul_kernel(x_ref, g_ref, sc_ref, sh_ref, w_ref, o_ref, *h_out):
    x = x_ref[0]
    inv = lax.rsqrt(jnp.mean(x * x, axis=-1, keepdims=True) + EPS)
    h = (x * inv) * g_ref[...]
    h = (h * (1.0 + sc_ref[0]) + sh_ref[0]).astype(BF16)
    for h_ref in h_out:
        h_ref[0] = h
    o_ref[0] = jnp.dot(h, w_ref[...], preferred_element_type=F32)


def _norm_mod_kernel(x_ref, g_ref, sc_ref, sh_ref, o_ref):
    x = x_ref[0]
    inv = lax.rsqrt(jnp.mean(x * x, axis=-1, keepdims=True) + EPS)
    h = (x * inv) * g_ref[...]
    o_ref[0] = (h * (1.0 + sc_ref[0]) + sh_ref[0]).astype(o_ref.dtype)


def _norm_mod(x, g_row, modr, row_scale, row_shift, tm=512):
    b, s, d = x.shape
    return pl.pallas_call(
        _norm_mod_kernel,
        name="norm_mod",
        grid=(b, s // tm),
        in_specs=[pl.BlockSpec((1, tm, d), lambda bi, i: (bi, i, 0)),
                  pl.BlockSpec((1, d), lambda bi, i: (0, 0)),
                  pl.BlockSpec((1, 1, d), lambda bi, i: (row_scale(bi), 0, 0)),
                  pl.BlockSpec((1, 1, d), lambda bi, i: (row_shift(bi), 0, 0))],
        out_specs=pl.BlockSpec((1, tm, d), lambda bi, i: (bi, i, 0)),
        out_shape=jax.ShapeDtypeStruct((b, s, d), BF16),
        compiler_params=_params(("parallel", "parallel")),
    )(x, g_row, modr, modr)


def _norm_matmul(x, g_row, modr, row_scale, row_shift, w, emit_h, tm=512):
    b, s, d = x.shape
    n = w.shape[1]
    rows = lambda width: pl.BlockSpec((1, tm, width), lambda bi, i: (bi, i, 0))
    out_specs, out_shape = [rows(n)], [jax.ShapeDtypeStruct((b, s, n), F32)]
    if emit_h:
        out_specs.append(rows(d))
        out_shape.append(jax.ShapeDtypeStruct((b, s, d), BF16))
    return pl.pallas_call(
        _norm_matmul_kernel,
        name="norm_matmul",
        grid=(b, s // tm),
        in_specs=[rows(d),
                  pl.BlockSpec((1, d), lambda bi, i: (0, 0)),
                  pl.BlockSpec((1, 1, d), lambda bi, i: (row_scale(bi), 0, 0)),
                  pl.BlockSpec((1, 1, d), lambda bi, i: (row_shift(bi), 0, 0)),
                  pl.BlockSpec((d, n), lambda bi, i: (0, 0))],
        out_specs=out_specs,
        out_shape=out_shape,
        compiler_params=_params(("parallel", "parallel")),
    )(x, g_row, modr, modr, w)


def _proj_res_kernel(a_ref, w_ref, x_ref, g_ref, o_ref):
    y = jnp.dot(a_ref[0], w_ref[...], preferred_element_type=F32)
    o_ref[0] = x_ref[0] + g_ref[0] * y


def _proj_residual(a, w, x, modr, row_gate, tm=512):
    b, s, k = a.shape
    d = w.shape[1]
    tn = d
    return pl.pallas_call(
        _proj_res_kernel,
        name="proj_residual",
        grid=(b, s // tm, d // tn),
        in_specs=[pl.BlockSpec((1, tm, k), lambda bi, i, j: (bi, i, 0)),
                  pl.BlockSpec((k, tn), lambda bi, i, j: (0, j)),
                  pl.BlockSpec((1, tm, tn), lambda bi, i, j: (bi, i, j)),
                  pl.BlockSpec((1, 1, tn), lambda bi, i, j: (row_gate(bi), 0, j))],
        out_specs=pl.BlockSpec((1, tm, tn), lambda bi, i, j: (bi, i, j)),
        out_shape=jax.ShapeDtypeStruct((b, s, d), F32),
        compiler_params=_params(("parallel", "parallel", "parallel")),
    )(a, w, x, modr)


def _mla_prep_kernel(z_ref, gq_ref, gkv_ref, wuq_ref, wukv_ref, gqn_ref, gkn_ref,
                     c_ref, s1_ref, s2_ref, q_ref, k_ref, vt_ref, *, q_scale):
    z = z_ref[0]
    cq = z[:, :MLA_Q_RANK]
    ckv = z[:, MLA_Q_RANK:MLA_Q_RANK + MLA_KV_RANK]
    kr = z[:, MLA_Q_RANK + MLA_KV_RANK:]

    def rms(t, g):
        return (t * lax.rsqrt(jnp.mean(t * t, axis=-1, keepdims=True) + EPS)) * g

    qa = jnp.dot(rms(cq, gq_ref[...]).astype(BF16), wuq_ref[...], preferred_element_type=F32)
    kva = jnp.dot(rms(ckv, gkv_ref[...]).astype(BF16), wukv_ref[...], preferred_element_type=F32)
    cos, sin_lo, sin_hi = c_ref[0], s1_ref[0], s2_ref[0]

    def rope(t):
        return (t * cos + pltpu.roll(t, LANES - MLA_ROPE // 2, 1) * sin_lo
                + pltpu.roll(t, MLA_ROPE // 2, 1) * sin_hi)

    gqn, gkn = gqn_ref[...], gkn_ref[...]
    kr_ss = jnp.sum(kr * kr, axis=-1, keepdims=True)
    kr_rot = rope(kr * gkn[:, MLA_NOPE:])
    for h in range(MLA_HEADS):
        lo = h * MLA_HEAD_PAD
        qh = qa[:, lo:lo + MLA_HEAD_PAD]
        inv = lax.rsqrt(jnp.sum(qh * qh, axis=-1, keepdims=True) * (1.0 / MLA_QK) + EPS) * q_scale
        q_ref[0, h, :, :MLA_NOPE] = ((qh[:, :MLA_NOPE] * gqn[:, :MLA_NOPE]) * inv).astype(BF16)
        q_ref[0, h, :, MLA_NOPE:] = (rope(qh[:, MLA_NOPE:] * gqn[:, MLA_NOPE:]) * inv).astype(BF16)
        kn = kva[:, lo:lo + MLA_NOPE]
        kss = jnp.sum(kn * kn, axis=-1, keepdims=True) + kr_ss
        kinv = lax.rsqrt(kss * (1.0 / MLA_QK) + EPS)
        k_ref[0, h, :, :MLA_NOPE] = ((kn * gkn[:, :MLA_NOPE]) * kinv).astype(BF16)
        k_ref[0, h, :, MLA_NOPE:] = (kr_rot * kinv).astype(BF16)
        vt_ref[0, h] = kva[:, lo + MLA_NOPE:lo + MLA_HEAD_PAD].T.astype(BF16)


def _mla_prep(z, g_q, g_kv, w_uq_p, w_ukv, g_qn_p, g_kn_p, tabs, tm=256):
    b, s, zw = z.shape
    nh = MLA_HEADS
    row = lambda w: pl.BlockSpec((1, w), lambda bi, i: (0, 0))
    full = lambda a: pl.BlockSpec(a.shape, lambda bi, i: (0, 0))
    tab = pl.BlockSpec((1, tm, LANES), lambda bi, i: (bi, i, 0))
    head = lambda w: pl.BlockSpec((1, nh, tm, w), lambda bi, i: (bi, 0, i, 0))
    kern = functools.partial(_mla_prep_kernel, q_scale=MLA_QK ** -0.5 * LOG2E)
    return pl.pallas_call(
        kern,
        name="mla_prep",
        grid=(b, s // tm),
        in_specs=[pl.BlockSpec((1, tm, zw), lambda bi, i: (bi, i, 0)),
                  row(MLA_Q_RANK), row(MLA_KV_RANK), full(w_uq_p), full(w_ukv),
                  row(MLA_HEAD_PAD), row(MLA_HEAD_PAD), tab, tab, tab],
        out_specs=[head(MLA_HEAD_PAD), head(MLA_HEAD_PAD),
                   pl.BlockSpec((1, nh, MLA_V, tm), lambda bi, i: (bi, 0, 0, i))],
        out_shape=[jax.ShapeDtypeStruct((b, nh, s, MLA_HEAD_PAD), BF16),
                   jax.ShapeDtypeStruct((b, nh, s, MLA_HEAD_PAD), BF16),
                   jax.ShapeDtypeStruct((b, nh, MLA_V, s), BF16)],
        compiler_params=_params(("parallel", "parallel")),
    )(z, g_q, g_kv, w_uq_p, w_ukv, g_qn_p, g_kn_p, *tabs)


def _flash_kernel(q_ref, k_ref, vt_ref, o_ref, acc_ref, *, t, nsub):
    qi = pl.program_id(2)
    acc_ref[...] = jnp.zeros(acc_ref.shape, F32)
    qs = [q_ref[0, 0, c * t:(c + 1) * t, :] for c in range(nsub)]

    def scores(c, j, diagonal):
        start = pl.multiple_of(j * t, t)
        k = k_ref[0, 0, pl.ds(start, t), :]
        st = lax.dot_general(k, qs[c], _NT, preferred_element_type=F32)
        if diagonal:
            key = lax.broadcasted_iota(jnp.int32, (t, t), 0)
            qry = lax.broadcasted_iota(jnp.int32, (t, t), 1)
            st = jnp.where(key <= qry, st, NEG_INF)
        return st

    def soft(st, m, l):
        m_new = jnp.maximum(m, jnp.max(st, axis=0, keepdims=True))
        p = jnp.exp2(st - m_new)
        alpha = jnp.exp2(m - m_new)
        l_new = alpha * l + jnp.sum(p, axis=0, keepdims=True)
        return p.astype(BF16), alpha, m_new, l_new

    def accum(c, j, p, alpha):
        start = pl.multiple_of(j * t, t)
        vt = vt_ref[0, 0, :, pl.ds(start, t)]
        acc_ref[c] = alpha * acc_ref[c] + jnp.dot(vt, p, preferred_element_type=F32)

    def multi(chains, j, carry, diag_chain):
        carry = list(carry)
        sts = [scores(c, j, c == diag_chain) for c in chains]
        ps = []
        for c, st in zip(chains, sts):
            p, alpha, carry[2 * c], carry[2 * c + 1] = soft(st, carry[2 * c], carry[2 * c + 1])
            ps.append((p, alpha))
        for c, (p, alpha) in zip(chains, ps):
            accum(c, j, p, alpha)
        return tuple(carry)

    def body(j, carry):
        return multi(range(nsub), j, carry, -1)

    init = []
    for c in range(nsub):
        init += [jnp.full((1, t), NEG_INF, F32), jnp.zeros((1, t), F32)]
    carry = lax.fori_loop(0, nsub * qi, body, tuple(init))
    for d in range(nsub):
        carry = multi(range(d, nsub), nsub * qi + d, carry, d)
    for c in range(nsub):
        ot = acc_ref[c] / carry[2 * c + 1]
        o_ref[0, c * t:(c + 1) * t, :] = ot.T.astype(o_ref.dtype)


def _flash_attention(q, k, vt, t=512, nsub=8):
    b, nh, s, dq = q.shape
    dv = vt.shape[2]
    tq = t * nsub
    return pl.pallas_call(
        functools.partial(_flash_kernel, t=t, nsub=nsub),
        name="mla_flash",
        grid=(b, nh, s // tq),
        in_specs=[pl.BlockSpec((1, 1, tq, dq), lambda bi, h, i: (bi, h, i, 0)),
                  pl.BlockSpec((1, 1, s, dq), lambda bi, h, i: (bi, h, 0, 0)),
                  pl.BlockSpec((1, 1, dv, s), lambda bi, h, i: (bi, h, 0, 0))],
        out_specs=pl.BlockSpec((1, tq, dv), lambda bi, h, i: (bi, i, h)),
        out_shape=jax.ShapeDtypeStruct((b, s, nh * dv), BF16),
        scratch_shapes=[pltpu.VMEM((nsub, dv, t), F32)],
        compiler_params=_params(("parallel", "parallel", "arbitrary")),
    )(q, k, vt)


_DIL_TN = 1024
_DIL_SUB = 256
_DIL_HPT = _DIL_TN // DIL_HEAD_DIM
_DIL_TPP = DIL_HEADS // _DIL_HPT


def _dil_proj_kernel(h_ref, w_ref, gain_ref, c_ref, s_ref, o_ref, *, q_scale):
    part = pl.program_id(3) // _DIL_TPP
    nsub = _DIL_TN // _DIL_SUB
    hps = _DIL_SUB // LANES
    h = h_ref[0]
    is_v = part == 2
    cos, sin = c_ref[0], s_ref[0]
    gain = gain_ref[0]
    post = jnp.where(part == 0, q_scale, 1.0)
    ri = lax.broadcasted_iota(jnp.int32, (_DIL_SUB, _DIL_SUB), 0) // LANES
    ci = lax.broadcasted_iota(jnp.int32, (_DIL_SUB, _DIL_SUB), 1) // LANES
    seg = jnp.where(ri == ci, 1.0, 0.0).astype(BF16)

    def main(sb):
        return jnp.dot(h, w_ref[:, sb * _DIL_SUB:(sb + 1) * _DIL_SUB], preferred_element_type=F32)

    def sumsq(y):
        return jnp.dot((y * y).astype(BF16), seg, preferred_element_type=F32)

    def finish(sb, y, ss):
        inv = lax.rsqrt(ss * (1.0 / DIL_HEAD_DIM) + EPS)
        for hh in range(hps):
            cols = slice(hh * LANES, (hh + 1) * LANES)
            yh = y[:, cols]
            yn = (yh * inv[:, cols]) * gain
            rot = yn * cos + pltpu.roll(yn, DIL_HEAD_DIM // 2, 1) * sin
            o_ref[0, 0, 0, sb * hps + hh] = jnp.where(is_v, yh, rot * post).astype(BF16)

    pairs = [(sb, sb + 1) for sb in range(0, nsub, 2)]
    ys = {sb: main(sb) for sb in pairs[0]}
    for n, pair in enumerate(pairs):
        ss = {sb: sumsq(ys[sb]) for sb in pair}
        if n + 1 < len(pairs):
            ys.update({sb: main(sb) for sb in pairs[n + 1]})
        for sb in pair:
            finish(sb, ys[sb], ss[sb])


def _dil_project(h, w_in, gains, cos_t, sin_t, group, dil):
    b, s, d = h.shape
    l = s // dil
    tl = min(1024, l)
    tiles = 3 * _DIL_TPP
    kern = functools.partial(_dil_proj_kernel, q_scale=DIL_HEAD_DIM ** -0.5 * LOG2E)
    return pl.pallas_call(
        kern,
        name=f"dil_proj_d{dil}",
        grid=(b, dil, l // tl, tiles),
        in_specs=[pl.BlockSpec((1, tl, d), lambda bi, r, i, j: (bi, i, r)),
                  pl.BlockSpec((d, _DIL_TN), lambda bi, r, i, j: (0, group * tiles + j)),
                  pl.BlockSpec((1, 1, LANES), lambda bi, r, i, j: (group * 3 + j // _DIL_TPP, 0, 0)),
                  pl.BlockSpec((1, tl, LANES), lambda bi, r, i, j: (bi, i, r)),
                  pl.BlockSpec((1, tl, LANES), lambda bi, r, i, j: (bi, i, r))],
        out_specs=pl.BlockSpec((1, 1, 1, _DIL_HPT, tl, LANES),
                               lambda bi, r, i, j: (bi, r, j // _DIL_TPP, j % _DIL_TPP, i, 0)),
        out_shape=jax.ShapeDtypeStruct((b, dil, 3, DIL_HEADS, l, LANES), BF16),
        compiler_params=_params(("parallel", "parallel", "parallel", "arbitrary")),
    )(h.reshape(b, l, dil * d), w_in, gains,
      cos_t.reshape(b, l, dil * LANES), sin_t.reshape(b, l, dil * LANES))


def _dil_attn_kernel(*refs, tq, has_state, is_last):
    q_ref, kc_ref, kp_ref, vc_ref, vp_ref = refs[:5]
    refs = refs[5:]
    if has_state:
        acc_in, m_in, l_in = refs[:3]
        refs = refs[3:]
    if is_last:
        (o_ref,) = refs
    else:
        acc_out, m_out, l_out = refs

    first = pl.program_id(2) == 0
    row = lax.broadcasted_iota(jnp.int32, (tq, tq), 0)
    col = lax.broadcasted_iota(jnp.int32, (tq, tq), 1)
    dist = row - col
    bias_c = jnp.where(dist >= 0, jnp.where(dist <= DIL_STEPS, 0.0, NEG_INF), NEG_INF)
    rowp = lax.broadcasted_iota(jnp.int32, (tq, DIL_STEPS), 0)
    colp = lax.broadcasted_iota(jnp.int32, (tq, DIL_STEPS), 1)
    bias_p = jnp.where(colp >= rowp, jnp.where(first, NEG_INF, 0.0), NEG_INF)
    lane = lax.broadcasted_iota(jnp.int32, (tq, LANES), 1)
    if has_state:
        m_old, l_old = m_in[0], l_in[0]
    m_tile = jnp.zeros((tq, LANES), F32)
    l_tile = jnp.zeros((tq, LANES), F32)

    for h in range(DIL_HEADS):
        q = q_ref[0, 0, 0, h]
        sc = lax.dot_general(q, kc_ref[0, 0, 0, h], _NT, preferred_element_type=F32) + bias_c
        sp = lax.dot_general(q, kp_ref[0, 0, 0, h], _NT, preferred_element_type=F32) + bias_p
        m_new = jnp.maximum(jnp.max(sc, axis=-1, keepdims=True), jnp.max(sp, axis=-1, keepdims=True))
        if has_state:
            m_prev = m_old[:, h:h + 1]
            m_new = jnp.maximum(m_new, m_prev)
            alpha = jnp.exp2(m_prev - m_new)
        pc = jnp.exp2(sc - m_new)
        pp = jnp.exp2(sp - m_new)
        l_new = jnp.sum(pc, axis=-1, keepdims=True) + jnp.sum(pp, axis=-1, keepdims=True)
        acc = (jnp.dot(pc.astype(BF16), vc_ref[0, 0, 0, h], preferred_element_type=F32)
               + jnp.dot(pp.astype(BF16), vp_ref[0, 0, 0, h], preferred_element_type=F32))
        cols = slice(h * LANES, (h + 1) * LANES)
        if has_state:
            l_new = l_new + alpha * l_old[:, h:h + 1]
            acc = acc + alpha * acc_in[0, :, cols]
        if is_last:
            o_ref[0, :, cols] = (acc / l_new).astype(o_ref.dtype)
        else:
            acc_out[0, :, cols] = acc
            m_tile = jnp.where(lane == h, m_new, m_tile)
            l_tile = jnp.where(lane == h, l_new, l_tile)
    if not is_last:
        m_out[0] = m_tile
        l_out[0] = l_tile


def _dil_attention(qkv, state, is_last, tq=512):
    b, dil, _, nh, l, dh = qkv.shape
    tq = min(tq, l)
    width = nh * dh
    ratio = tq // DIL_STEPS
    cur = lambda part: pl.BlockSpec((1, 1, 1, nh, tq, dh), lambda bi, r, i: (bi, r, part, 0, i, 0))
    prev = lambda part: pl.BlockSpec(
        (1, 1, 1, nh, DIL_STEPS, dh),
        lambda bi, r, i: (bi, r, part, 0, jnp.maximum(i * ratio - 1, 0), 0))
    wide = pl.BlockSpec((1, tq, width), lambda bi, r, i: (bi, i, r))
    thin = pl.BlockSpec((1, tq, LANES), lambda bi, r, i: (bi, i, r))
    in_specs = [cur(0), cur(1), prev(1), cur(2), prev(2)]
    args = [qkv] * 5
    if state is not None:
        in_specs += [wide, thin, thin]
        args += [state[0].reshape(b, l, dil * width), state[1].reshape(b, l, dil * LANES),
                 state[2].reshape(b, l, dil * LANES)]
    if is_last:
        out_specs = wide
        out_shape = jax.ShapeDtypeStruct((b, l, dil * width), BF16)
    else:
        out_specs = [wide, thin, thin]
        out_shape = [jax.ShapeDtypeStruct((b, l, dil * width), F32),
                     jax.ShapeDtypeStruct((b, l, dil * LANES), F32),
                     jax.ShapeDtypeStruct((b, l, dil * LANES), F32)]
    kern = functools.partial(_dil_attn_kernel, tq=tq, has_state=state is not None, is_last=is_last)
    out = pl.pallas_call(
        kern,
        name=f"dil_attn_d{dil}",
        grid=(b, dil, l // tq),
        in_specs=in_specs,
        out_specs=out_specs,
        out_shape=out_shape,
        compiler_params=_params(("parallel", "parallel", "parallel")),
    )(*args)
    s = l * dil
    if is_last:
        return out.reshape(b, s, width)
    return (out[0].reshape(b, s, width), out[1].reshape(b, s, LANES), out[2].reshape(b, s, LANES))


def _batcher_pairs(n):
    pairs = []
    p = 1
    while p < n:
        k = p
        while k >= 1:
            for j in range(k % p, n - k, 2 * k):
                for i in range(min(k, n - j - k)):
                    if (i + j) // (2 * p) == (i + j + k) // (2 * p):
                        pairs.append((i + j, i + j + k))
            k //= 2
        p *= 2
    return pairs


def _sorted_top(vs, count):
    vs = list(vs) + [None] * (count - len(vs))

    def exchange(i, j):
        a, b = vs[i], vs[j]
        if b is None:
            return
        if a is None:
            vs[i], vs[j] = b, None
            return
        vs[i], vs[j] = jnp.maximum(a, b), jnp.minimum(a, b)

    for i, j in _batcher_pairs(count):
        exchange(i, j)
    shape = next(v.shape for v in vs if v is not None)
    vs = [jnp.full(shape, NEG_INF, F32) if v is None else v for v in vs]
    for shift in (4, 2, 1):
        other = [pltpu.roll(v, shift, 0) for v in vs]
        vs = [jnp.maximum(vs[i], other[count - 1 - i]) for i in range(count)]
        k = count // 2
        while k >= 1:
            for i in range(count):
                if i & k == 0:
                    exchange(i, i + k)
            k //= 2
    return vs


def _peer_topk_kernel(q_ref, keys_ref, rk2_ref, c1_ref, e1_ref, e2_ref):
    k = PEER_TOPK
    tt = q_ref.shape[0]
    nslab = PEER_NKEYS // 8
    sub = lax.broadcasted_iota(jnp.int32, (8, tt), 0)

    def by_sublane(vals):
        out = vals[0]
        for i in range(1, 8):
            out = jnp.where(sub == i, vals[i], out)
        return out

    def count(slab, thr):
        return jnp.sum(jnp.where(slab >= thr, 1.0, 0.0), axis=0, keepdims=True)

    for h in range(PEER_HEADS):
        sts, tops = [], []
        for p in range(2):
            hp = 2 * h + p
            qs = q_ref[:, hp * PEER_NKEYS:(hp + 1) * PEER_NKEYS]
            st = lax.dot_general(keys_ref[hp], qs, _NT, preferred_element_type=F32,
                                 precision=lax.Precision.HIGHEST)
            slabs = [st[8 * i:8 * i + 8, :] for i in range(nslab)]
            sts.append(slabs)
            tops.append(_sorted_top(slabs, k))
        a, bv = tops
        b_lo, b_hi, a_hi = by_sublane(bv[:8]), by_sublane(bv[8:]), by_sublane(a[8:])
        cand = [a[0] + b_lo, a[0] + b_hi, a[1] + b_lo]
        for i in range(2, 8):
            cand.append(jnp.where(sub < k // (i + 1), a[i] + b_lo, NEG_INF))
        cand.append(a_hi + bv[0])
        best = _sorted_top(cand, k)
        top, thr = best[0], best[k - 1]
        z = jnp.exp(best[0] - top)
        for r in range(1, k):
            z = z + jnp.exp(best[r] - top)
        cnt = [count(cand[0], thr) + count(cand[1], thr)]
        cnt += [count(cand[i + 1], thr) for i in range(1, 8)]
        tail = jnp.where(cand[9] >= thr, 1.0, 0.0)
        cnt += [tail[i:i + 1] for i in range(8)]
        inv_z = 1.0 / z
        for i in range(nslab):
            s1, s2 = sts[0][i], sts[1][i]
            c1 = jnp.zeros(s1.shape, F32)
            rk = jnp.full(s2.shape, float(k), F32)
            for r in range(k - 1, -1, -1):
                c1 = jnp.where(s1 >= a[r], cnt[r], c1)
                rk = jnp.where(s2 >= bv[r], float(r), rk)
            rows = slice(8 * i, 8 * i + 8)
            c1_ref[h, rows, :] = c1
            rk2_ref[h, rows, :] = rk.astype(BF16)
            e1_ref[h, rows, :] = jnp.exp(s1 - a[0]) * inv_z
            e2_ref[h, rows, :] = jnp.exp(s2 - bv[0]).astype(BF16)


def _peer_topk(q2d, sub_keys, tt=512):
    t = q2d.shape[0]
    nh = PEER_HEADS
    keys = sub_keys.reshape(2 * nh, PEER_NKEYS, sub_keys.shape[-1])
    big = pl.BlockSpec((nh, PEER_NKEYS, tt), lambda i: (0, 0, i))
    return pl.pallas_call(
        _peer_topk_kernel,
        name="peer_topk",
        grid=(t // tt,),
        in_specs=[pl.BlockSpec((tt, q2d.shape[1]), lambda i: (i, 0)),
                  pl.BlockSpec(keys.shape, lambda i: (0, 0, 0))],
        out_specs=[big, big, big, big],
        out_shape=[jax.ShapeDtypeStruct((nh, PEER_NKEYS, t), BF16),
                   jax.ShapeDtypeStruct((nh, PEER_NKEYS, t), F32),
                   jax.ShapeDtypeStruct((nh, PEER_NKEYS, t), F32),
                   jax.ShapeDtypeStruct((nh, PEER_NKEYS, t), BF16)],
        compiler_params=_params(("parallel",)),
    )(q2d, keys)


def _peer_dense_kernel(x_ref, u_ref, vt_ref, rk2_ref, c1_ref, e1_ref, e2_ref, o_ref, act_ref, *, te):
    j = pl.program_id(1)

    @pl.when(j == 0)
    def _():
        o_ref[...] = jnp.zeros(o_ref.shape, F32)

    half = te // 2
    per = half // PEER_NKEYS
    pre = [lax.dot_general(u_ref[sb * half:(sb + 1) * half, :], x_ref[...], _NT,
                           preferred_element_type=F32) for sb in range(2)]
    for sb in range(2):
        a = pre[sb]
        ge = (0.5 * a * (1.0 + lax.erf(a * (1.0 / math.sqrt(2.0))))).astype(BF16)
        for ii in range(per):
            i1 = j * (2 * per) + sb * per + ii
            gate = jnp.zeros((PEER_NKEYS, x_ref.shape[0]), BF16)
            for h in range(PEER_HEADS):
                c1 = c1_ref[h, pl.ds(i1, 1), :].astype(BF16)
                e1 = e1_ref[h, pl.ds(i1, 1), :].astype(BF16)
                gate = gate + jnp.where(rk2_ref[h] < c1, e2_ref[h], jnp.zeros((), BF16)) * e1
            lo = sb * half + ii * PEER_NKEYS
            act_ref[lo:lo + PEER_NKEYS, :] = ge[ii * PEER_NKEYS:(ii + 1) * PEER_NKEYS, :] * gate
    o_ref[...] += jnp.dot(vt_ref[...], act_ref[...], preferred_element_type=F32)


def _peer_dense(h2d, u_bf, vt_bf, rk2, c1, e1, e2, tt=512, te=2048):
    t, d = h2d.shape
    ne = u_bf.shape[0]
    once = pl.Buffered(1)
    big = pl.BlockSpec((PEER_HEADS, PEER_NKEYS, tt), lambda i, j: (0, 0, i), pipeline_mode=once)
    return pl.pallas_call(
        functools.partial(_peer_dense_kernel, te=te),
        name="peer_dense",
        grid=(t // tt, ne // te),
        in_specs=[pl.BlockSpec((tt, d), lambda i, j: (i, 0), pipeline_mode=once),
                  pl.BlockSpec((te, d), lambda i, j: (j, 0)),
                  pl.BlockSpec((d, te), lambda i, j: (0, j)),
                  big, big, big, big],
        out_specs=pl.BlockSpec((d, tt), lambda i, j: (0, i)),
        out_shape=jax.ShapeDtypeStruct((d, t), F32),
        scratch_shapes=[pltpu.VMEM((te, tt), BF16)],
        compiler_params=_params(("parallel", "arbitrary"), vmem_mb=58),
    )(h2d, u_bf, vt_bf, rk2, c1, e1, e2)


def _res_t_kernel(x_ref, yt_ref, g_ref, o_ref):
    o_ref[0] = x_ref[0] + g_ref[0] * yt_ref[...].T


def _residual_t(x, y_t, modr, row_gate, tm=512, tn=512):
    b, s, d = x.shape
    nt = s // tm
    return pl.pallas_call(
        _res_t_kernel,
        name="residual_t",
        grid=(b, nt, d // tn),
        in_specs=[pl.BlockSpec((1, tm, tn), lambda bi, i, j: (bi, i, j)),
                  pl.BlockSpec((tn, tm), lambda bi, i, j: (j, bi * nt + i)),
                  pl.BlockSpec((1, 1, tn), lambda bi, i, j: (row_gate(bi), 0, j))],
        out_specs=pl.BlockSpec((1, tm, tn), lambda bi, i, j: (bi, i, j)),
        out_shape=jax.ShapeDtypeStruct((b, s, d), F32),
        compiler_params=_params(("parallel", "parallel", "parallel")),
    )(x, y_t, modr)


def _pad_cols(a, width):
    return jnp.pad(a, ((0, 0), (0, width - a.shape[1])))


def _mla_weights(w_in, w_uq, g_qn, g_kn):
    w_in_p = _pad_cols(w_in, MLA_Q_RANK + MLA_KV_RANK + LANES).astype(BF16)
    w_uq_p = jnp.pad(w_uq.reshape(MLA_Q_RANK, MLA_HEADS, MLA_QK),
                     ((0, 0), (0, 0), (0, MLA_HEAD_PAD - MLA_QK)))
    w_uq_p = w_uq_p.reshape(MLA_Q_RANK, MLA_HEADS * MLA_HEAD_PAD).astype(BF16)
    pad = lambda g: _pad_cols(g.reshape(1, MLA_QK), MLA_HEAD_PAD)
    return w_in_p, w_uq_p, pad(g_qn), pad(g_kn)


def _peer(x, layer, norm_row, modr, row, w_q, sub_keys, u_tab, v_tab):
    b, s, d = x.shape
    q, hmod = _norm_matmul(x, norm_row, modr, row(layer, 4), row(layer, 3), w_q.astype(BF16), emit_h=True)
    rk2, c1, e1, e2 = _peer_topk(q.reshape(b * s, -1), sub_keys)
    y_t = _peer_dense(hmod.reshape(b * s, d), u_tab.astype(BF16), v_tab.T.astype(BF16), rk2, c1, e1, e2)
    return _residual_t(x, y_t, modr, row(layer, 5))


def kernel(x, c, positions, ada_w, ada_b, norm_g, mla_w_in, mla_g_q, mla_w_uq, mla_g_kv, mla_w_ukv, mla_g_qn, mla_g_kn, mla_w_o, dil_w_in, dil_g_qn, dil_g_kn, dil_w_o, peer_w_q, peer_sub_keys, peer_u, peer_v):
    b, s, d = x.shape
    depth = ada_w.shape[0]
    mod = _ada_mod(c, ada_w, ada_b)
    modr = mod.reshape(depth * b * 6, 1, d)
    row = lambda layer, part: (lambda bi: (layer * b + bi) * 6 + part)
    m_cos, m_sin_lo, m_sin_hi, d_cos, d_sin = _rope_tables(positions)

    for layer in range(depth):
        g1 = norm_g[layer, 0].reshape(1, d)
        g2 = norm_g[layer, 1].reshape(1, d)
        a = layer // 2
        if layer % 2 == 0:
            w_in_p, w_uq_p, g_qn_p, g_kn_p = _mla_weights(mla_w_in[a], mla_w_uq[a], mla_g_qn[a], mla_g_kn[a])
            (z,) = _norm_matmul(x, g1, modr, row(layer, 1), row(layer, 0), w_in_p, emit_h=False)
            q, k, v = _mla_prep(z, mla_g_q[a].reshape(1, -1), mla_g_kv[a].reshape(1, -1), w_uq_p,
                                mla_w_ukv[a].astype(BF16), g_qn_p, g_kn_p, (m_cos, m_sin_lo, m_sin_hi))
            o = _flash_attention(q, k, v)
            x = _proj_residual(o, mla_w_o[a].astype(BF16), x, modr, row(layer, 2))
        else:
            hmod = _norm_mod(x, g1, modr, row(layer, 1), row(layer, 0))
            w_in = dil_w_in[a].astype(BF16)
            ones = jnp.ones((len(DIL_GROUPS), DIL_HEAD_DIM), F32)
            gains = jnp.stack([dil_g_qn[a], dil_g_kn[a], ones], axis=1).reshape(-1, 1, DIL_HEAD_DIM)
            state = None
            for gi in reversed(range(len(DIL_GROUPS))):
                dil = DIL_GROUPS[gi][1]
                qkv = _dil_project(hmod, w_in, gains, d_cos, d_sin, gi, dil)
                state = _dil_attention(qkv, state, is_last=gi == 0)
            x = _proj_residual(state, dil_w_o[a].astype(BF16), x, modr, row(layer, 2))
        x = _peer(x, layer, g2, modr, row, peer_w_q[layer], peer_sub_keys[layer], peer_u[layer], peer_v[layer])
    return x
```
